```python
import jax
import jax.numpy as jnp
from jax import lax
import numpy as np

D_MODEL = 1024
BATCH = 2
SEQ = 8192
DEPTH = 2

RET_HEADS = 4
RET_DK = 128
RET_DV = 256
RET_CHUNK = 128
SWA_Q_HEADS = 16
SWA_KV_HEADS = 2
SWA_HEAD_DIM = 64
WINDOW = 128
SWA_BLOCK = 128
D_FF = 2816
ROPE_THETA = 10000.0
NORM_EPS = 1e-6
MAX_POS_OFFSET = 1024
N_MOD = 9

RET_QK_WIDTH = RET_HEADS * RET_DK
RET_V_WIDTH = RET_HEADS * RET_DV
SWA_Q_WIDTH = SWA_Q_HEADS * SWA_HEAD_DIM
SWA_KV_WIDTH = SWA_KV_HEADS * SWA_HEAD_DIM
IN_SPLITS = (RET_QK_WIDTH, RET_QK_WIDTH, RET_V_WIDTH, RET_V_WIDTH, SWA_Q_WIDTH, SWA_KV_WIDTH, SWA_KV_WIDTH, D_MODEL, D_MODEL)
IN_COLS = sum(IN_SPLITS)

kernel_name = 'hybrid_retention_swa_macaron_adaln'


def rmsnorm(x, gain):
    xf = x.astype(jnp.float32)
    y = xf * lax.rsqrt(jnp.mean(xf * xf, axis=-1, keepdims=True) + NORM_EPS)
    return (y * gain.astype(jnp.float32)).astype(x.dtype)


def modulate(h, shift, scale):
    return h * (1 + scale[:, None, :]) + shift[:, None, :]


def swiglu(h, w_up, w_down):
    gate, up = jnp.split(h @ w_up, 2, axis=-1)
    return (jax.nn.silu(gate) * up) @ w_down


def rope_tables(positions, dim):
    inv_freq = 1.0 / (ROPE_THETA ** (jnp.arange(0, dim, 2, dtype=jnp.float32) / dim))
    ang = positions.astype(jnp.float32)[..., None] * inv_freq
    return jnp.cos(ang), jnp.sin(ang)


def apply_rope(x, cos, sin):
    xf = x.astype(jnp.float32)
    x1, x2 = jnp.split(xf, 2, axis=-1)
    c = cos[:, :, None, :]
    s = sin[:, :, None, :]
    return jnp.concatenate([x1 * c - x2 * s, x2 * c + x1 * s], axis=-1).astype(x.dtype)


def head_groupnorm(y):
    mu = jnp.mean(y, axis=-1, keepdims=True)
    var = jnp.mean(jnp.square(y - mu), axis=-1, keepdims=True)
    return (y - mu) * lax.rsqrt(var + NORM_EPS)


def chunkwise_retention(q, k, v):
    B, T, H, dk = q.shape
    dv = v.shape[-1]
    C = RET_CHUNK
    N = T // C
    f32 = jnp.float32
    log_gamma = jnp.log1p(-(2.0 ** (-5.0 - jnp.arange(H, dtype=f32))))
    qc = (q.astype(f32) * (dk ** -0.5)).reshape(B, N, C, H, dk)
    kc = k.astype(f32).reshape(B, N, C, H, dk)
    vc = v.astype(f32).reshape(B, N, C, H, dv)
    idx = jnp.arange(C, dtype=f32)
    rel = idx[:, None] - idx[None, :]
    decay_intra = jnp.where(rel[None] >= 0,
                            jnp.exp(jnp.maximum(rel, 0.0)[None] * log_gamma[:, None, None]),
                            0.0)
    scores = jnp.einsum('bnihd,bnjhd->bnhij', qc, kc) * decay_intra[None, None]
    intra = jnp.einsum('bnhij,bnjhe->bnihe', scores, vc)
    zeta = jnp.exp((C - 1 - idx)[None, :] * log_gamma[:, None])
    xi = jnp.exp((idx + 1)[None, :] * log_gamma[:, None])
    chunk_kv = jnp.einsum('bnjhd,hj,bnjhe->nbhde', kc, zeta, vc)
    gamma_chunk = jnp.exp(C * log_gamma)[None, :, None, None]

    def step(state, kv_n):
        return state * gamma_chunk + kv_n, state

    _, prev_states = lax.scan(step, jnp.zeros((B, H, dk, dv), f32), chunk_kv)
    cross = jnp.einsum('bnihd,nbhde,hi->bnihe', qc, prev_states, xi)
    return (intra + cross).reshape(B, T, H, dv)


def sliding_window_gqa_sinks(q, k, v, sinks):
    B, T, Hq, d = q.shape
    Hkv = k.shape[2]
    G = Hq // Hkv
    C = SWA_BLOCK
    N = T // C
    qb = q.reshape(B, N, C, Hkv, G, d)
    kb = k.reshape(B, N, C, Hkv, d)
    vb = v.reshape(B, N, C, Hkv, d)
    pad = jnp.zeros_like(kb[:, :1])
    k2 = jnp.concatenate([jnp.concatenate([pad, kb[:, :-1]], axis=1), kb], axis=2)
    v2 = jnp.concatenate([jnp.concatenate([pad, vb[:, :-1]], axis=1), vb], axis=2)
    scores = jnp.einsum('bnikgd,bnjkd->bkgnij', qb, k2).astype(jnp.float32) * (d ** -0.5)
    qi = jnp.arange(C)[:, None] + C
    kj = jnp.arange(2 * C)[None, :]
    rel = qi - kj
    band = (rel >= 0) & (rel < WINDOW)
    blk_valid = (jnp.arange(N)[:, None, None] > 0) | (kj >= C)[None]
    mask = band[None] & blk_valid
    scores = jnp.where(mask, scores, -jnp.inf)
    sink = jnp.broadcast_to(sinks.astype(jnp.float32).reshape(Hkv, G)[None, :, :, None, None, None],
                            scores.shape[:-1] + (1,))
    logits = jnp.concatenate([scores, sink], axis=-1)
    m = jnp.max(logits, axis=-1, keepdims=True)
    p = jnp.exp(logits - m)
    probs = p[..., :-1] / jnp.sum(p, axis=-1, keepdims=True)
    out = jnp.einsum('bkgnij,bnjkd->bnikgd', probs.astype(v.dtype), v2)
    return out.reshape(B, T, Hq * d)


def hybrid_mixer(h, cos_r, sin_r, cos_s, sin_s, w_in, sinks, w_branch_ret, w_branch_swa, w_out):
    B, T, _ = h.shape
    offsets = np.cumsum(IN_SPLITS)[:-1]
    rq, rk, rv, rg, sq, sk, sv, gate_r, gate_s = jnp.split(h @ w_in, offsets, axis=-1)
    rq = apply_rope(rq.reshape(B, T, RET_HEADS, RET_DK), cos_r, sin_r)
    rk = apply_rope(rk.reshape(B, T, RET_HEADS, RET_DK), cos_r, sin_r)
    ret = chunkwise_retention(rq, rk, rv.reshape(B, T, RET_HEADS, RET_DV))
    ret = head_groupnorm(ret).reshape(B, T, RET_V_WIDTH).astype(h.dtype)
    ret = jax.nn.silu(rg) * ret
    branch_ret = ret @ w_branch_ret
    sq = apply_rope(sq.reshape(B, T, SWA_Q_HEADS, SWA_HEAD_DIM), cos_s, sin_s)
    sk = apply_rope(sk.reshape(B, T, SWA_KV_HEADS, SWA_HEAD_DIM), cos_s, sin_s)
    sv = sv.reshape(B, T, SWA_KV_HEADS, SWA_HEAD_DIM)
    branch_swa = sliding_window_gqa_sinks(sq, sk, sv, sinks) @ w_branch_swa
    merged = jax.nn.sigmoid(gate_r) * branch_ret + jax.nn.sigmoid(gate_s) * branch_swa
    return merged @ w_out


def setup_inputs(seed: int = 0) -> dict:
    key = jax.random.key(seed)
    ks = jax.random.split(key, 20)
    f32 = jnp.float32

    def dense(k, shape, fan_in, mult=1.0):
        return jax.random.normal(k, shape, f32) * (mult * fan_in ** -0.5)

    def gains(k, shape):
        return 1.0 + 0.05 * jax.random.normal(k, shape, f32)

    x = jax.random.normal(ks[0], (BATCH, SEQ, D_MODEL), f32)
    c = jax.random.normal(ks[1], (BATCH, D_MODEL), f32)
    offset = jax.random.randint(ks[2], (BATCH, 1), 0, MAX_POS_OFFSET, dtype=jnp.int32)
    positions = (offset + jnp.arange(SEQ, dtype=jnp.int32)[None, :]).astype(jnp.int32)
    return {
        'x': x,
        'c': c,
        'positions': positions,
        'norm_ffn1': gains(ks[3], (DEPTH, D_MODEL)),
        'norm_mix': gains(ks[4], (DEPTH, D_MODEL)),
        'norm_ffn2': gains(ks[5], (DEPTH, D_MODEL)),
        'final_norm': gains(ks[6], (D_MODEL,)),
        'w_ada': dense(ks[7], (DEPTH, D_MODEL, N_MOD * D_MODEL), D_MODEL, 0.5),
        'b_ada': 0.02 * jax.random.normal(ks[8], (DEPTH, N_MOD * D_MODEL), f32),
        'ffn1_w_up': dense(ks[9], (DEPTH, D_MODEL, 2 * D_FF), D_MODEL),
        'ffn1_w_down': dense(ks[10], (DEPTH, D_FF, D_MODEL), D_FF),
        'ffn2_w_up': dense(ks[11], (DEPTH, D_MODEL, 2 * D_FF), D_MODEL),
        'ffn2_w_down': dense(ks[12], (DEPTH, D_FF, D_MODEL), D_FF),
        'w_in': dense(ks[13], (DEPTH, D_MODEL, IN_COLS), D_MODEL),
        'sinks': jax.random.normal(ks[14], (DEPTH, SWA_Q_HEADS), f32),
        'w_branch_ret': dense(ks[15], (DEPTH, RET_V_WIDTH, D_MODEL), RET_V_WIDTH),
        'w_branch_swa': dense(ks[16], (DEPTH, SWA_Q_WIDTH, D_MODEL), SWA_Q_WIDTH),
        'w_out': dense(ks[17], (DEPTH, D_MODEL, D_MODEL), D_MODEL),
    }


def reference(x, c, positions, norm_ffn1, norm_mix, norm_ffn2, final_norm, w_ada, b_ada,
              ffn1_w_up, ffn1_w_down, ffn2_w_up, ffn2_w_down, w_in, sinks,
              w_branch_ret, w_branch_swa, w_out):
    cos_r, sin_r = rope_tables(positions, RET_DK)
    cos_s, sin_s = rope_tables(positions, SWA_HEAD_DIM)
    c_act = jax.nn.silu(c)
    for l in range(DEPTH):
        mod = c_act @ w_ada[l] + b_ada[l]
        sh1, sc1, g1, sh2, sc2, g2, sh3, sc3, g3 = jnp.split(mod, N_MOD, axis=-1)
        h = modulate(rmsnorm(x, norm_ffn1[l]), sh1, sc1)
        x = x + 0.5 * g1[:, None, :] * swiglu(h, ffn1_w_up[l], ffn1_w_down[l])
        h = modulate(rmsnorm(x, norm_mix[l]), sh2, sc2)
        x = x + g2[:, None, :] * hybrid_mixer(h, cos_r, sin_r, cos_s, sin_s, w_in[l], sinks[l],
                                              w_branch_ret[l], w_branch_swa[l], w_out[l])
        h = modulate(rmsnorm(x, norm_ffn2[l]), sh3, sc3)
        x = x + 0.5 * g3[:, None, :] * swiglu(h, ffn2_w_up[l], ffn2_w_down[l])
    return rmsnorm(x, final_norm)
```

```python
import functools

import numpy as np
import jax
import jax.numpy as jnp
from jax import lax
from jax.experimental import pallas as pl
from jax.experimental.pallas import tpu as pltpu

F32 = jnp.float32
BF16 = jnp.bfloat16

D_MODEL = 1024
DEPTH = 2
RET_HEADS = 4
RET_DK = 128
RET_DV = 256
CHUNK = 128
SWA_Q_HEADS = 16
SWA_KV_HEADS = 2
SWA_HEAD_DIM = 64
D_FF = 2816
ROPE_THETA = 10000.0
NORM_EPS = 1e-6
N_MOD = 9

RET_QK_WIDTH = RET_HEADS * RET_DK
RET_V_WIDTH = RET_HEADS * RET_DV
SWA_Q_WIDTH = SWA_Q_HEADS * SWA_HEAD_DIM
SWA_KV_WIDTH = SWA_KV_HEADS * SWA_HEAD_DIM
OFF_RQ = 0
OFF_RK = OFF_RQ + RET_QK_WIDTH
OFF_RV = OFF_RK + RET_QK_WIDTH
OFF_RG = OFF_RV + RET_V_WIDTH
OFF_SQ = OFF_RG + RET_V_WIDTH
OFF_SK = OFF_SQ + SWA_Q_WIDTH
OFF_SV = OFF_SK + SWA_KV_WIDTH
OFF_GR = OFF_SV + SWA_KV_WIDTH
OFF_GS = OFF_GR + D_MODEL
IN_COLS = OFF_GS + D_MODEL

LANES = 128
ROPE_COLS = 5 * LANES

TOKEN_TILE = 512
SEQ_TILE = 512
FF_CHUNK = 256
ADA_COLS = 1152
VMEM_LIMIT = 56 * 1024 * 1024


def _sigmoid(v):
    return 1.0 / (1.0 + jnp.exp(-v))


def _dot(a, b):
    return jnp.dot(a, b, preferred_element_type=F32)


def _dot_nt(a, b):
    return lax.dot_general(a, b, (((1,), (1,)), ((), ())), preferred_element_type=F32)


def _resident(shape):
    zeros = (0,) * len(shape)
    return pl.BlockSpec(shape, lambda *_: zeros, pipeline_mode=pl.Buffered(1))


def _params():
    return pltpu.CompilerParams(dimension_semantics=("arbitrary",), vmem_limit_bytes=VMEM_LIMIT)


def _params2():
    return pltpu.CompilerParams(dimension_semantics=("arbitrary", "arbitrary"),
                                vmem_limit_bytes=VMEM_LIMIT)


def _norm_mod(x, gain, shift, scale):
    y = x * lax.rsqrt(jnp.mean(x * x, axis=-1, keepdims=True) + NORM_EPS)
    return (y * gain) * (1.0 + scale) + shift


def _ada_kernel(c_ref, w_ref, b_ref, o_ref):
    c = c_ref[...]
    act = (c * _sigmoid(c)).astype(BF16)
    o_ref[...] = _dot(act, w_ref[...].astype(BF16)) + b_ref[...]


def _ada(c_pad, w_ada, b_ada):
    rows = c_pad.shape[0]
    cols = N_MOD * D_MODEL
    return pl.pallas_call(
        _ada_kernel,
        grid=(DEPTH, cols // ADA_COLS),
        in_specs=[
            pl.BlockSpec((rows, D_MODEL), lambda l, j: (0, 0)),
            pl.BlockSpec((None, D_MODEL, ADA_COLS), lambda l, j: (l, 0, j)),
            pl.BlockSpec((None, 1, ADA_COLS), lambda l, j: (l, 0, j)),
        ],
        out_specs=pl.BlockSpec((None, rows, ADA_COLS), lambda l, j: (l, 0, j)),
        out_shape=jax.ShapeDtypeStruct((DEPTH, rows, cols), F32),
        compiler_params=_params2(),
        name="ada",
    )(c_pad, w_ada, b_ada.reshape(DEPTH, 1, cols))


def _rope_kernel(pos_ref, invf_ref, o_ref):
    ang = pos_ref[...] * invf_ref[...]
    c = jnp.cos(ang)
    s = jnp.sin(ang)
    lane = lax.broadcasted_iota(jnp.int32, ang.shape, 1)
    c64 = pltpu.roll(c, 64, 1)
    s64 = pltpu.roll(s, 64, 1)
    o_ref[:, 0:LANES] = jnp.where(lane < 64, c, c64)
    o_ref[:, LANES:2 * LANES] = jnp.where(lane < 64, -s, s64)

    def tile4(v, v64):
        return jnp.where(lane < 32, v64,
                         jnp.where(lane < 64, pltpu.roll(v, 96, 1),
                                   jnp.where(lane < 96, v, pltpu.roll(v, 32, 1))))

    cs = tile4(c, c64)
    ss = tile4(s, s64)
    first_half = (lane & 63) < 32
    o_ref[:, 2 * LANES:3 * LANES] = cs
    o_ref[:, 3 * LANES:4 * LANES] = jnp.where(first_half, -ss, 0.0)
    o_ref[:, 4 * LANES:5 * LANES] = jnp.where(first_half, 0.0, ss)


def _rope_tables(positions):
    n = positions.size
    inv_r = 1.0 / (ROPE_THETA ** (jnp.arange(0, RET_DK, 2, dtype=F32) / RET_DK))
    inv_s = 1.0 / (ROPE_THETA ** (jnp.arange(0, SWA_HEAD_DIM, 2, dtype=F32) / SWA_HEAD_DIM))
    invf = jnp.concatenate([inv_r, inv_s, jnp.zeros((LANES - 96,), F32)]).reshape(1, LANES)
    pos = positions.astype(F32).reshape(n, 1)
    tm = 1024
    return pl.pallas_call(
        _rope_kernel,
        grid=(n // tm,),
        in_specs=[pl.BlockSpec((tm, 1), lambda i: (i, 0)),
                  pl.BlockSpec((1, LANES), lambda i: (0, 0))],
        out_specs=pl.BlockSpec((tm, ROPE_COLS), lambda i: (i, 0)),
        out_shape=jax.ShapeDtypeStruct((n, ROPE_COLS), F32),
        compiler_params=_params(),
        name="rope",
    )(pos, invf)


def _ffn_kernel(mod_row, final, x_ref, gain_ref, mod_ref, wup_ref, wdn_ref, *rest):
    if final:
        fgain_ref, o_ref, act_ref = rest
    else:
        o_ref, act_ref = rest
    x = x_ref[...]
    shift = mod_ref[mod_row:mod_row + 1, :]
    scale = mod_ref[mod_row + 1:mod_row + 2, :]
    gate = mod_ref[mod_row + 2:mod_row + 3, :]
    hb = _norm_mod(x, gain_ref[...], shift, scale).astype(BF16)
    for j in range(D_FF // FF_CHUNK):
        lo = j * FF_CHUNK
        g = _dot(hb, wup_ref[:, lo:lo + FF_CHUNK])
        u = _dot(hb, wup_ref[:, D_FF + lo:D_FF + lo + FF_CHUNK])
        act_ref[:, lo:lo + FF_CHUNK] = (g * _sigmoid(g) * u).astype(BF16)
    y = _dot(act_ref[...], wdn_ref[...])
    out = x + (0.5 * gate) * y
    if final:
        out = out * lax.rsqrt(jnp.mean(out * out, axis=-1, keepdims=True) + NORM_EPS)
        out = out * fgain_ref[...]
    o_ref[...] = out


def _ffn(x, gain, mod, layer, mod_row, w_up, w_down, batch, final_gain=None):
    n = x.shape[0]
    tm = TOKEN_TILE
    tiles_per_batch = n // batch // tm
    final = final_gain is not None
    in_specs = [
        pl.BlockSpec((tm, D_MODEL), lambda i: (i, 0)),
        _resident((1, D_MODEL)),
        pl.BlockSpec((None, N_MOD, D_MODEL), lambda i: (layer * batch + i // tiles_per_batch, 0, 0)),
        _resident((D_MODEL, 2 * D_FF)),
        _resident((D_FF, D_MODEL)),
    ]
    args = [x, gain.reshape(1, D_MODEL), mod, w_up, w_down]
    if final:
        in_specs.append(_resident((1, D_MODEL)))
        args.append(final_gain.reshape(1, D_MODEL))
    return pl.pallas_call(
        functools.partial(_ffn_kernel, mod_row, final),
        grid=(n // tm,),
        in_specs=in_specs,
        out_specs=pl.BlockSpec((tm, D_MODEL), lambda i: (i, 0)),
        out_shape=jax.ShapeDtypeStruct((n, D_MODEL), F32),
        scratch_shapes=[pltpu.VMEM((tm, D_FF), BF16)],
        compiler_params=_params(),
        name="ffn",
    )(*args)


def _proj_kernel(x_ref, gain_ref, mod_ref, rope_ref, w_ref,
                 rq_ref, rk_ref, rv_ref, rg_ref, sq_ref, skv_ref, gr_ref, gs_ref):
    shift = mod_ref[3:4, :]
    scale = mod_ref[4:5, :]
    hb = _norm_mod(x_ref[...], gain_ref[...], shift, scale).astype(BF16)
    cos_r = rope_ref[:, 0:LANES]
    sin_r = rope_ref[:, LANES:2 * LANES]
    cos_s = rope_ref[:, 2 * LANES:3 * LANES]
    sin_a = rope_ref[:, 3 * LANES:4 * LANES]
    sin_b = rope_ref[:, 4 * LANES:5 * LANES]

    def rope_ret(v):
        return v * cos_r + pltpu.roll(v, 64, 1) * sin_r

    def rope_swa(v):
        return v * cos_s + pltpu.roll(v, 96, 1) * sin_a + pltpu.roll(v, 32, 1) * sin_b

    def project(off, width, out_ref, out_off=0, rope=None):
        y = _dot(hb, w_ref[:, off:off + width])
        if rope is None:
            out_ref[:, out_off:out_off + width] = y.astype(out_ref.dtype)
        else:
            for s in range(width // LANES):
                slab = y[:, s * LANES:(s + 1) * LANES]
                out_ref[:, out_off + s * LANES:out_off + (s + 1) * LANES] = rope(slab).astype(out_ref.dtype)

    project(OFF_RQ, RET_QK_WIDTH, rq_ref, rope=rope_ret)
    project(OFF_RK, RET_QK_WIDTH, rk_ref, rope=rope_ret)
    project(OFF_RV, RET_V_WIDTH, rv_ref)
    project(OFF_RG, RET_V_WIDTH, rg_ref)
    project(OFF_SQ, SWA_Q_WIDTH, sq_ref, rope=rope_swa)
    project(OFF_SK, SWA_KV_WIDTH, skv_ref, rope=rope_swa)
    project(OFF_SV, SWA_KV_WIDTH, skv_ref, out_off=SWA_KV_WIDTH)
    project(OFF_GR, D_MODEL, gr_ref)
    project(OFF_GS, D_MODEL, gs_ref)


def _proj(x, gain, mod, layer, rope, w_in, batch):
    n = x.shape[0]
    tm = TOKEN_TILE
    tiles_per_batch = n // batch // tm
    widths = (RET_QK_WIDTH, RET_QK_WIDTH, RET_V_WIDTH, RET_V_WIDTH, SWA_Q_WIDTH,
              2 * SWA_KV_WIDTH, D_MODEL, D_MODEL)
    return pl.pallas_call(
        _proj_kernel,
        grid=(n // tm,),
        in_specs=[
            pl.BlockSpec((tm, D_MODEL), lambda i: (i, 0)),
            _resident((1, D_MODEL)),
            pl.BlockSpec((None, N_MOD, D_MODEL), lambda i: (layer * batch + i // tiles_per_batch, 0, 0)),
            pl.BlockSpec((tm, ROPE_COLS), lambda i: (i, 0)),
            _resident((D_MODEL, IN_COLS)),
        ],
        out_specs=[pl.BlockSpec((tm, w), lambda i: (i, 0)) for w in widths],
        out_shape=[jax.ShapeDtypeStruct((n, w), BF16) for w in widths],
        compiler_params=_params(),
        name="proj",
    )(x, gain.reshape(1, D_MODEL), mod, rope, w_in)


def _ret_tables():
    h = np.arange(RET_HEADS, dtype=np.float64)
    log_gamma = np.log1p(-(2.0 ** (-5.0 - h)))
    idx = np.arange(CHUNK, dtype=np.float64)
    rel = idx[:, None] - idx[None, :]
    decay = np.where(rel[None] >= 0, np.exp(np.maximum(rel, 0.0)[None] * log_gamma[:, None, None]), 0.0)
    decay = decay * (RET_DK ** -0.5)
    zeta = np.exp((CHUNK - 1 - idx)[None, :] * log_gamma[:, None])
    xi = np.exp((idx + 1)[None, :] * log_gamma[:, None]) * (RET_DK ** -0.5)
    zeta = np.broadcast_to(zeta[:, :, None], (RET_HEADS, CHUNK, RET_DK))
    xi = np.broadcast_to(xi[:, :, None], (RET_HEADS, CHUNK, RET_DV))
    gamma_chunk = tuple(float(v) for v in np.exp(CHUNK * log_gamma))
    return (jnp.asarray(decay, F32), jnp.asarray(zeta, F32), jnp.asarray(xi, F32), gamma_chunk)


def _ret_kernel(gamma_chunk, q_ref, k_ref, v_ref, g_ref, dec_ref, zeta_ref, xi_ref, o_ref, state_ref):
    @pl.when(pl.program_id(1) == 0)
    def _():
        state_ref[...] = jnp.zeros_like(state_ref)

    for c in range(SEQ_TILE // CHUNK):
        rows = slice(c * CHUNK, (c + 1) * CHUNK)
        for h in range(RET_HEADS):
            kcols = slice(h * RET_DK, (h + 1) * RET_DK)
            vcols = slice(h * RET_DV, (h + 1) * RET_DV)
            q = q_ref[rows, kcols]
            k = k_ref[rows, kcols]
            v = v_ref[rows, vcols]
            state = state_ref[h]
            scores = _dot_nt(q, k) * dec_ref[h]
            out = _dot(scores.astype(BF16), v) + _dot(q, state.astype(BF16)) * xi_ref[h]
            mu = jnp.mean(out, axis=-1, keepdims=True)
            dev = out - mu
            var = jnp.mean(dev * dev, axis=-1, keepdims=True)
            y = dev * lax.rsqrt(var + NORM_EPS)
            gate = g_ref[rows, vcols].astype(F32)
            o_ref[rows, vcols] = (gate * _sigmoid(gate) * y).astype(BF16)
            kz = (k.astype(F32) * zeta_ref[h]).astype(BF16)
            state_ref[h] = state * gamma_chunk[h] + _dot(kz.T, v)


def _retention(rq, rk, rv, rg, batch):
    n = rq.shape[0]
    steps = n // batch // SEQ_TILE
    decay, zeta, xi, gamma_chunk = _ret_tables()
    row_block = lambda w: pl.BlockSpec((SEQ_TILE, w), lambda b, t: (b * steps + t, 0))
    return pl.pallas_call(
        functools.partial(_ret_kernel, gamma_chunk),
        grid=(batch, steps),
        in_specs=[row_block(RET_QK_WIDTH), row_block(RET_QK_WIDTH), row_block(RET_V_WIDTH),
                  row_block(RET_V_WIDTH),
                  _resident((RET_HEADS, CHUNK, CHUNK)), _resident((RET_HEADS, CHUNK, RET_DK)),
                  _resident((RET_HEADS, CHUNK, RET_DV))],
        out_specs=row_block(RET_V_WIDTH),
        out_shape=jax.ShapeDtypeStruct((n, RET_V_WIDTH), BF16),
        scratch_shapes=[pltpu.VMEM((RET_HEADS, RET_DK, RET_DV), F32)],
        compiler_params=_params2(),
        name="retention",
    )(rq, rk, rv, rg, decay, zeta, xi)


def _swa_kernel(sink_ref, q_ref, kv_ref, kvp_ref, o_ref):
    first_step = pl.program_id(1) == 0
    lane = lax.broadcasted_iota(jnp.int32, (2 * CHUNK, LANES), 1)
    low = lane < SWA_HEAD_DIM
    low_q = lax.broadcasted_iota(jnp.int32, (CHUNK, LANES), 1) < SWA_HEAD_DIM
    row = lax.broadcasted_iota(jnp.int32, (4 * CHUNK, LANES), 0)
    lane4 = lax.broadcasted_iota(jnp.int32, (4 * CHUNK, LANES), 1)
    ones_pair = jnp.where((row < 2 * CHUNK) == (lane4 < SWA_HEAD_DIM), 1.0, 0.0).astype(BF16)
    qi = lax.broadcasted_iota(jnp.int32, (CHUNK, 4 * CHUNK), 0)
    col = lax.broadcasted_iota(jnp.int32, (CHUNK, 4 * CHUNK), 1)
    kj = col & (2 * CHUNK - 1)
    rel = qi + CHUNK - kj
    band = (rel >= 0) & (rel < CHUNK)
    head0 = col < 2 * CHUNK
    scale = SWA_HEAD_DIM ** -0.5

    def pair_blocks(m):
        mf = m.astype(F32)
        sw = pltpu.roll(mf, SWA_HEAD_DIM, 1)
        zero = jnp.zeros_like(mf)
        g0 = jnp.concatenate([jnp.where(low, mf, zero), jnp.where(low, zero, sw)], axis=0)
        g1 = jnp.concatenate([jnp.where(low, sw, zero), jnp.where(low, zero, mf)], axis=0)
        return g0.astype(BF16), g1.astype(BF16)

    for j in range(SEQ_TILE // CHUNK):
        rows = slice(j * CHUNK, (j + 1) * CHUNK)
        prev = kvp_ref[...] if j == 0 else kv_ref[(j - 1) * CHUNK:j * CHUNK, :]
        kv2 = jnp.concatenate([prev, kv_ref[rows, :]], axis=0)
        kb = pair_blocks(kv2[:, :LANES])
        vb = pair_blocks(kv2[:, LANES:])
        valid = band
        if j == 0:
            valid = band & (jnp.logical_not(first_step) | (kj >= CHUNK))
        for p in range(SWA_Q_HEADS // 2):
            g = p // (SWA_Q_HEADS // 2 // SWA_KV_HEADS)
            sink0 = sink_ref[2 * p]
            sink1 = sink_ref[2 * p + 1]
            qp = q_ref[rows, p * LANES:(p + 1) * LANES]
            s = _dot_nt(qp, kb[g]) * scale
            s = jnp.where(valid, s, -jnp.inf)
            m0 = jnp.maximum(jnp.max(s[:, :2 * CHUNK], axis=-1, keepdims=True), sink0)
            m1 = jnp.maximum(jnp.max(s[:, 2 * CHUNK:], axis=-1, keepdims=True), sink1)
            pexp = jnp.exp(s - jnp.where(head0, m0, m1)).astype(BF16)
            num = _dot(pexp, vb[g])
            den = _dot(pexp, ones_pair) + jnp.where(low_q, jnp.exp(sink0 - m0), jnp.exp(sink1 - m1))
            o_ref[rows, p * LANES:(p + 1) * LANES] = (num / den).astype(BF16)


def _swa(sinks, sq, skv, batch):
    n = sq.shape[0]
    steps = n // batch // SEQ_TILE
    per = SEQ_TILE // CHUNK
    return pl.pallas_call(
        _swa_kernel,
        grid=(batch, steps),
        in_specs=[
            pl.BlockSpec(memory_space=pltpu.SMEM),
            pl.BlockSpec((SEQ_TILE, SWA_Q_WIDTH), lambda b, t: (b * steps + t, 0)),
            pl.BlockSpec((SEQ_TILE, 2 * SWA_KV_WIDTH), lambda b, t: (b * steps + t, 0)),
            pl.BlockSpec((CHUNK, 2 * SWA_KV_WIDTH),
                         lambda b, t: ((b * steps + t) * per - jnp.minimum(t, 1), 0)),
        ],
        out_specs=pl.BlockSpec((SEQ_TILE, SWA_Q_WIDTH), lambda b, t: (b * steps + t, 0)),
        out_shape=jax.ShapeDtypeStruct((n, SWA_Q_WIDTH), BF16),
        compiler_params=_params2(),
        name="swa",
    )(sinks, sq, skv, skv)


def _merge_kernel(x_ref, mod_ref, ret_ref, swa_ref, gr_ref, gs_ref, wr_ref, ws_ref, wo_ref, o_ref):
    br = _dot(ret_ref[...], wr_ref[...])
    bs = _dot(swa_ref[...], ws_ref[...])
    merged = _sigmoid(gr_ref[...].astype(F32)) * br + _sigmoid(gs_ref[...].astype(F32)) * bs
    y = _dot(merged.astype(BF16), wo_ref[...])
    o_ref[...] = x_ref[...] + mod_ref[5:6, :] * y


def _merge(x, mod, layer, ret, swa, gr, gs, w_ret, w_swa, w_out, batch):
    n = x.shape[0]
    tm = TOKEN_TILE
    tiles_per_batch = n // batch // tm
    rows = lambda w: pl.BlockSpec((tm, w), lambda i: (i, 0))
    return pl.pallas_call(
        _merge_kernel,
        grid=(n // tm,),
        in_specs=[
            rows(D_MODEL),
            pl.BlockSpec((None, N_MOD, D_MODEL), lambda i: (layer * batch + i // tiles_per_batch, 0, 0)),
            rows(RET_V_WIDTH), rows(SWA_Q_WIDTH), rows(D_MODEL), rows(D_MODEL),
            _resident((RET_V_WIDTH, D_MODEL)), _resident((SWA_Q_WIDTH, D_MODEL)),
            _resident((D_MODEL, D_MODEL)),
        ],
        out_specs=rows(D_MODEL),
        out_shape=jax.ShapeDtypeStruct((n, D_MODEL), F32),
        compiler_params=_params(),
        name="merge",
    )(x, mod, ret, swa, gr, gs, w_ret, w_swa, w_out)


def kernel(x, c, positions, norm_ffn1, norm_mix, norm_ffn2, final_norm, w_ada, b_ada,
           ffn1_w_up, ffn1_w_down, ffn2_w_up, ffn2_w_down, w_in, sinks,
           w_branch_ret, w_branch_swa, w_out):
    batch, seq, d = x.shape
    n = batch * seq
    assert d == D_MODEL and seq % SEQ_TILE == 0 and seq % TOKEN_TILE == 0
    xf = x.reshape(n, d)

    pad_rows = -batch % 8
    c_pad = jnp.pad(c, ((0, pad_rows), (0, 0)))
    mod = _ada(c_pad, w_ada, b_ada)[:, :batch].reshape(DEPTH * batch, N_MOD, D_MODEL)
    rope = _rope_tables(positions)

    for l in range(DEPTH):
        xf = _ffn(xf, norm_ffn1[l], mod, l, 0, ffn1_w_up[l].astype(BF16), ffn1_w_down[l].astype(BF16), batch)
        rq, rk, rv, rg, sq, skv, gr, gs = _proj(xf, norm_mix[l], mod, l, rope, w_in[l].astype(BF16), batch)
        ret = _retention(rq, rk, rv, rg, batch)
        swa = _swa(sinks[l], sq, skv, batch)
        xf = _merge(xf, mod, l, ret, swa, gr, gs, w_branch_ret[l].astype(BF16),
                    w_branch_swa[l].astype(BF16), w_out[l].astype(BF16), batch)
        final_gain = final_norm if l == DEPTH - 1 else None
        xf = _ffn(xf, norm_ffn2[l], mod, l, 6, ffn2_w_up[l].astype(BF16), ffn2_w_down[l].astype(BF16),
                  batch, final_gain)
    return xf.reshape(batch, seq, d)
```

```python
import functools

import numpy as np
import jax
import jax.numpy as jnp
from jax import lax
from jax.experimental import pallas as pl
from jax.experimental.pallas import tpu as pltpu

F32 = jnp.float32
BF16 = jnp.bfloat16

D_MODEL = 1024
DEPTH = 2
RET_HEADS = 4
RET_DK = 128
RET_DV = 256
CHUNK = 128
SWA_Q_HEADS = 16
SWA_KV_HEADS = 2
SWA_HEAD_DIM = 64
D_FF = 2816
ROPE_THETA = 10000.0
NORM_EPS = 1e-6
N_MOD = 9

RET_QK_WIDTH = RET_HEADS * RET_DK
RET_V_WIDTH = RET_HEADS * RET_DV
SWA_Q_WIDTH = SWA_Q_HEADS * SWA_HEAD_DIM
SWA_KV_WIDTH = SWA_KV_HEADS * SWA_HEAD_DIM
OFF_RQ = 0
OFF_RK = OFF_RQ + RET_QK_WIDTH
OFF_RV = OFF_RK + RET_QK_WIDTH
OFF_RG = OFF_RV + RET_V_WIDTH
OFF_SQ = OFF_RG + RET_V_WIDTH
OFF_SK = OFF_SQ + SWA_Q_WIDTH
OFF_SV = OFF_SK + SWA_KV_WIDTH
OFF_GR = OFF_SV + SWA_KV_WIDTH
OFF_GS = OFF_GR + D_MODEL
IN_COLS = OFF_GS + D_MODEL

LANES = 128
ROPE_COLS = 5 * LANES

TOKEN_TILE = 512
SEQ_TILE = 512
FF_CHUNK = 256
ADA_COLS = 1152
VMEM_LIMIT = 56 * 1024 * 1024


def _sigmoid(v):
    return 1.0 / (1.0 + jnp.exp(-v))


def _dot(a, b):
    return jnp.dot(a, b, preferred_element_type=F32)


def _dot_nt(a, b):
    return lax.dot_general(a, b, (((1,), (1,)), ((), ())), preferred_element_type=F32)


def _resident(shape):
    zeros = (0,) * len(shape)
    return pl.BlockSpec(shape, lambda *_: zeros, pipeline_mode=pl.Buffered(1))


def _layer_resident(layer, shape):
    zeros = (0,) * len(shape)
    return pl.BlockSpec((None,) + tuple(shape), lambda *_: (layer,) + zeros,
                        pipeline_mode=pl.Buffered(1))


def _wdot(a, w):
    return jnp.dot(a, w.astype(BF16), preferred_element_type=F32)


def _params():
    return pltpu.CompilerParams(dimension_semantics=("arbitrary",), vmem_limit_bytes=VMEM_LIMIT)


def _params2():
    return pltpu.CompilerParams(dimension_semantics=("arbitrary", "arbitrary"),
                                vmem_limit_bytes=VMEM_LIMIT)


def _norm_mod(x, gain, shift, scale):
    y = x * lax.rsqrt(jnp.mean(x * x, axis=-1, keepdims=True) + NORM_EPS)
    return (y * gain) * (1.0 + scale) + shift


def _ada_kernel(c_ref, w_ref, b_ref, o_ref):
    c = c_ref[...]
    act = (c * _sigmoid(c)).astype(BF16)
    o_ref[...] = _dot(act, w_ref[...].astype(BF16)) + b_ref[...]


def _ada(c_pad, w_ada, b_ada):
    rows = c_pad.shape[0]
    cols = N_MOD * D_MODEL
    return pl.pallas_call(
        _ada_kernel,
        grid=(DEPTH, cols // ADA_COLS),
        in_specs=[
            pl.BlockSpec((rows, D_MODEL), lambda l, j: (0, 0)),
            pl.BlockSpec((None, D_MODEL, ADA_COLS), lambda l, j: (l, 0, j)),
            pl.BlockSpec((None, 1, ADA_COLS), lambda l, j: (l, 0, j)),
        ],
        out_specs=pl.BlockSpec((None, rows, ADA_COLS), lambda l, j: (l, 0, j)),
        out_shape=jax.ShapeDtypeStruct((DEPTH, rows, cols), F32),
        compiler_params=_params2(),
        name="ada",
    )(c_pad, w_ada, b_ada.reshape(DEPTH, 1, cols))


def _rope_kernel(pos_ref, invf_ref, o_ref):
    ang = pos_ref[...] * invf_ref[...]
    c = jnp.cos(ang)
    s = jnp.sin(ang)
    lane = lax.broadcasted_iota(jnp.int32, ang.shape, 1)
    c64 = pltpu.roll(c, 64, 1)
    s64 = pltpu.roll(s, 64, 1)
    o_ref[:, 0:LANES] = jnp.where(lane < 64, c, c64)
    o_ref[:, LANES:2 * LANES] = jnp.where(lane < 64, -s, s64)

    def tile4(v, v64):
        return jnp.where(lane < 32, v64,
                         jnp.where(lane < 64, pltpu.roll(v, 96, 1),
                                   jnp.where(lane < 96, v, pltpu.roll(v, 32, 1))))

    cs = tile4(c, c64)
    ss = tile4(s, s64)
    first_half = (lane & 63) < 32
    o_ref[:, 2 * LANES:3 * LANES] = cs
    o_ref[:, 3 * LANES:4 * LANES] = jnp.where(first_half, -ss, 0.0)
    o_ref[:, 4 * LANES:5 * LANES] = jnp.where(first_half, 0.0, ss)


def _rope_tables(positions):
    n = positions.size
    inv_r = 1.0 / (ROPE_THETA ** (jnp.arange(0, RET_DK, 2, dtype=F32) / RET_DK))
    inv_s = 1.0 / (ROPE_THETA ** (jnp.arange(0, SWA_HEAD_DIM, 2, dtype=F32) / SWA_HEAD_DIM))
    invf = jnp.concatenate([inv_r, inv_s, jnp.zeros((LANES - 96,), F32)]).reshape(1, LANES)
    pos = positions.astype(F32).reshape(n, 1)
    tm = 1024
    return pl.pallas_call(
        _rope_kernel,
        grid=(n // tm,),
        in_specs=[pl.BlockSpec((tm, 1), lambda i: (i, 0)),
                  pl.BlockSpec((1, LANES), lambda i: (0, 0))],
        out_specs=pl.BlockSpec((tm, ROPE_COLS), lambda i: (i, 0)),
        out_shape=jax.ShapeDtypeStruct((n, ROPE_COLS), F32),
        compiler_params=_params(),
        name="rope",
    )(pos, invf)


def _ffn_kernel(mod_row, final, x_ref, gain_ref, mod_ref, wup_ref, wdn_ref, *rest):
    if final:
        fgain_ref, o_ref, act_ref = rest
    else:
        o_ref, act_ref = rest
    x = x_ref[...]
    shift = mod_ref[mod_row:mod_row + 1, :]
    scale = mod_ref[mod_row + 1:mod_row + 2, :]
    gate = mod_ref[mod_row + 2:mod_row + 3, :]
    hb = _norm_mod(x, gain_ref[...], shift, scale).astype(BF16)
    for j in range(D_FF // FF_CHUNK):
        lo = j * FF_CHUNK
        g = _wdot(hb, wup_ref[:, lo:lo + FF_CHUNK])
        u = _wdot(hb, wup_ref[:, D_FF + lo:D_FF + lo + FF_CHUNK])
        act_ref[:, lo:lo + FF_CHUNK] = (g * _sigmoid(g) * u).astype(BF16)
    y = _wdot(act_ref[...], wdn_ref[...])
    out = x + (0.5 * gate) * y
    if final:
        out = out * lax.rsqrt(jnp.mean(out * out, axis=-1, keepdims=True) + NORM_EPS)
        out = out * fgain_ref[...]
    o_ref[...] = out


def _ffn(x, gain, mod, layer, mod_row, w_up, w_down, batch, final_gain=None):
    n = x.shape[0]
    tm = TOKEN_TILE
    tiles_per_batch = n // batch // tm
    final = final_gain is not None
    in_specs = [
        pl.BlockSpec((tm, D_MODEL), lambda i: (i, 0)),
        _resident((1, D_MODEL)),
        pl.BlockSpec((None, N_MOD, D_MODEL), lambda i: (layer * batch + i // tiles_per_batch, 0, 0)),
        _layer_resident(layer, (D_MODEL, 2 * D_FF)),
        _layer_resident(layer, (D_FF, D_MODEL)),
    ]
    args = [x, gain.reshape(1, D_MODEL), mod, w_up, w_down]
    if final:
        in_specs.append(_resident((1, D_MODEL)))
        args.append(final_gain.reshape(1, D_MODEL))
    return pl.pallas_call(
        functools.partial(_ffn_kernel, mod_row, final),
        grid=(n // tm,),
        in_specs=in_specs,
        out_specs=pl.BlockSpec((tm, D_MODEL), lambda i: (i, 0)),
        out_shape=jax.ShapeDtypeStruct((n, D_MODEL), F32),
        scratch_shapes=[pltpu.VMEM((tm, D_FF), BF16)],
        compiler_params=_params(),
        name="ffn",
    )(*args)


def _proj_kernel(x_ref, gain_ref, mod_ref, rope_ref, w_ref,
                 rq_ref, rk_ref, rv_ref, rg_ref, sq_ref, skv_ref, gr_ref, gs_ref):
    shift = mod_ref[3:4, :]
    scale = mod_ref[4:5, :]
    hb = _norm_mod(x_ref[...], gain_ref[...], shift, scale).astype(BF16)
    cos_r = rope_ref[:, 0:LANES]
    sin_r = rope_ref[:, LANES:2 * LANES]
    cos_s = rope_ref[:, 2 * LANES:3 * LANES]
    sin_a = rope_ref[:, 3 * LANES:4 * LANES]
    sin_b = rope_ref[:, 4 * LANES:5 * LANES]

    def rope_ret(v):
        return v * cos_r + pltpu.roll(v, 64, 1) * sin_r

    def rope_swa(v):
        return v * cos_s + pltpu.roll(v, 96, 1) * sin_a + pltpu.roll(v, 32, 1) * sin_b

    def rope_swa_q(v):
        return rope_swa(v) * (SWA_HEAD_DIM ** -0.5)

    def project(off, width, out_ref, out_off=0, rope=None):
        y = _wdot(hb, w_ref[:, off:off + width])
        if rope is None:
            out_ref[:, out_off:out_off + width] = y.astype(out_ref.dtype)
        else:
            for s in range(width // LANES):
                slab = y[:, s * LANES:(s + 1) * LANES]
                out_ref[:, out_off + s * LANES:out_off + (s + 1) * LANES] = rope(slab).astype(out_ref.dtype)

    project(OFF_RQ, RET_QK_WIDTH, rq_ref, rope=rope_ret)
    project(OFF_RK, RET_QK_WIDTH, rk_ref, rope=rope_ret)
    project(OFF_RV, RET_V_WIDTH, rv_ref)
    project(OFF_RG, RET_V_WIDTH, rg_ref)
    project(OFF_SQ, SWA_Q_WIDTH, sq_ref, rope=rope_swa_q)
    project(OFF_SK, SWA_KV_WIDTH, skv_ref, rope=rope_swa)
    project(OFF_SV, SWA_KV_WIDTH, skv_ref, out_off=SWA_KV_WIDTH)
    project(OFF_GR, D_MODEL, gr_ref)
    project(OFF_GS, D_MODEL, gs_ref)


def _proj(x, gain, mod, layer, rope, w_in, batch):
    n = x.shape[0]
    tm = TOKEN_TILE
    tiles_per_batch = n // batch // tm
    widths = (RET_QK_WIDTH, RET_QK_WIDTH, RET_V_WIDTH, RET_V_WIDTH, SWA_Q_WIDTH,
              2 * SWA_KV_WIDTH, D_MODEL, D_MODEL)
    return pl.pallas_call(
        _proj_kernel,
        grid=(n // tm,),
        in_specs=[
            pl.BlockSpec((tm, D_MODEL), lambda i: (i, 0)),
            _resident((1, D_MODEL)),
            pl.BlockSpec((None, N_MOD, D_MODEL), lambda i: (layer * batch + i // tiles_per_batch, 0, 0)),
            pl.BlockSpec((tm, ROPE_COLS), lambda i: (i, 0)),
            _layer_resident(layer, (D_MODEL, IN_COLS)),
        ],
        out_specs=[pl.BlockSpec((tm, w), lambda i: (i, 0)) for w in widths],
        out_shape=[jax.ShapeDtypeStruct((n, w), BF16) for w in widths],
        compiler_params=_params(),
        name="proj",
    )(x, gain.reshape(1, D_MODEL), mod, rope, w_in)


def _ret_tables():
    h = np.arange(RET_HEADS, dtype=np.float64)
    log_gamma = np.log1p(-(2.0 ** (-5.0 - h)))
    idx = np.arange(CHUNK, dtype=np.float64)
    rel = idx[:, None] - idx[None, :]
    decay = np.where(rel[None] >= 0, np.exp(np.maximum(rel, 0.0)[None] * log_gamma[:, None, None]), 0.0)
    decay = decay * (RET_DK ** -0.5)
    zeta = np.exp((CHUNK - 1 - idx)[None, :] * log_gamma[:, None])
    xi = np.exp((idx + 1)[None, :] * log_gamma[:, None]) * (RET_DK ** -0.5)
    zeta = np.broadcast_to(zeta[:, :, None], (RET_HEADS, CHUNK, RET_DK))
    xi = np.broadcast_to(xi[:, :, None], (RET_HEADS, CHUNK, RET_DV))
    gamma_chunk = tuple(float(v) for v in np.exp(CHUNK * log_gamma))
    return (jnp.asarray(decay, F32), jnp.asarray(zeta, F32), jnp.asarray(xi, F32), gamma_chunk)


def _ret_kernel(gamma_chunk, q_ref, k_ref, v_ref, g_ref, dec_ref, zeta_ref, xi_ref, o_ref, state_ref):
    @pl.when(pl.program_id(1) == 0)
    def _():
        state_ref[...] = jnp.zeros_like(state_ref)

    for c in range(SEQ_TILE // CHUNK):
        rows = slice(c * CHUNK, (c + 1) * CHUNK)
        for h in range(RET_HEADS):
            kcols = slice(h * RET_DK, (h + 1) * RET_DK)
            vcols = slice(h * RET_DV, (h + 1) * RET_DV)
            q = q_ref[rows, kcols]
            k = k_ref[rows, kcols]
            v = v_ref[rows, vcols]
            state = state_ref[h]
            scores = _dot_nt(q, k) * dec_ref[h]
            out = _dot(scores.astype(BF16), v) + _dot(q, state.astype(BF16)) * xi_ref[h]
            mu = jnp.mean(out, axis=-1, keepdims=True)
            dev = out - mu
            var = jnp.mean(dev * dev, axis=-1, keepdims=True)
            y = dev * lax.rsqrt(var + NORM_EPS)
            gate = g_ref[rows, vcols].astype(F32)
            o_ref[rows, vcols] = (gate * _sigmoid(gate) * y).astype(BF16)
            kz = (k.astype(F32) * zeta_ref[h]).astype(BF16)
            state_ref[h] = state * gamma_chunk[h] + _dot(kz.T, v)


def _retention(rq, rk, rv, rg, batch):
    n = rq.shape[0]
    steps = n // batch // SEQ_TILE
    decay, zeta, xi, gamma_chunk = _ret_tables()
    row_block = lambda w: pl.BlockSpec((SEQ_TILE, w), lambda b, t: (b * steps + t, 0))
    return pl.pallas_call(
        functools.partial(_ret_kernel, gamma_chunk),
        grid=(batch, steps),
        in_specs=[row_block(RET_QK_WIDTH), row_block(RET_QK_WIDTH), row_block(RET_V_WIDTH),
                  row_block(RET_V_WIDTH),
                  _resident((RET_HEADS, CHUNK, CHUNK)), _resident((RET_HEADS, CHUNK, RET_DK)),
                  _resident((RET_HEADS, CHUNK, RET_DV))],
        out_specs=row_block(RET_V_WIDTH),
        out_shape=jax.ShapeDtypeStruct((n, RET_V_WIDTH), BF16),
        scratch_shapes=[pltpu.VMEM((RET_HEADS, RET_DK, RET_DV), F32)],
        compiler_params=_params2(),
        name="retention",
    )(rq, rk, rv, rg, decay, zeta, xi)


def _swa_kernel(sink_ref, q_ref, kv_ref, kvp_ref, o_ref):
    first_step = pl.program_id(1) == 0
    low = lax.broadcasted_iota(jnp.int32, (CHUNK, LANES), 1) < SWA_HEAD_DIM
    qi = lax.broadcasted_iota(jnp.int32, (CHUNK, CHUNK), 0)
    kj = lax.broadcasted_iota(jnp.int32, (CHUNK, CHUNK), 1)
    tri = kj <= qi
    zero = jnp.zeros((CHUNK, LANES), F32)
    ones_lo = jnp.where(low, 1.0, 0.0)
    ones_hi = jnp.where(low, 0.0, 1.0)

    def pieces(kvb):
        kf = kvb[:, :LANES].astype(F32)
        vf = kvb[:, LANES:].astype(F32)
        ks = pltpu.roll(kf, SWA_HEAD_DIM, 1)
        vs = pltpu.roll(vf, SWA_HEAD_DIM, 1)
        out = []
        for lo_src_k, hi_src_k, lo_src_v, hi_src_v in ((kf, ks, vf, vs), (ks, kf, vs, vf)):
            k_lo = jnp.where(low, lo_src_k, zero).astype(BF16)
            k_hi = jnp.where(low, zero, hi_src_k).astype(BF16)
            vo_lo = jnp.concatenate([jnp.where(low, lo_src_v, zero), ones_lo], axis=1).astype(BF16)
            vo_hi = jnp.concatenate([jnp.where(low, zero, hi_src_v), ones_hi], axis=1).astype(BF16)
            out.append((k_lo, k_hi, vo_lo, vo_hi))
        return out

    prev = pieces(kvp_ref[...])
    for j in range(SEQ_TILE // CHUNK):
        rows = slice(j * CHUNK, (j + 1) * CHUNK)
        cur = pieces(kv_ref[rows, :])
        kb = [jnp.concatenate([prev[g][0], cur[g][0], prev[g][1], cur[g][1]], axis=0)
              for g in range(SWA_KV_HEADS)]
        vo = [jnp.concatenate([prev[g][2], cur[g][2], prev[g][3], cur[g][3]], axis=0)
              for g in range(SWA_KV_HEADS)]
        for p in range(SWA_Q_HEADS // 2):
            g = p // (SWA_Q_HEADS // 2 // SWA_KV_HEADS)
            s = _dot_nt(q_ref[rows, p * LANES:(p + 1) * LANES], kb[g])
            probs = []
            sink_terms = []
            for hh in range(2):
                sink = sink_ref[2 * p + hh]
                s_prev = s[:, (2 * hh) * CHUNK:(2 * hh + 1) * CHUNK]
                s_cur = s[:, (2 * hh + 1) * CHUNK:(2 * hh + 2) * CHUNK]
                if j == 0:
                    s_prev = jnp.where(first_step, -jnp.inf, s_prev)
                comb = jnp.where(tri, s_cur, s_prev)
                m = jnp.maximum(jnp.max(comb, axis=-1, keepdims=True), sink)
                pe = jnp.exp(comb - m)
                p_cur = jnp.where(tri, pe, 0.0)
                probs += [pe - p_cur, p_cur]
                sink_terms.append(jnp.exp(sink - m))
            res = _dot(jnp.concatenate(probs, axis=1).astype(BF16), vo[g])
            den = res[:, LANES:] + jnp.where(low, sink_terms[0], sink_terms[1])
            o_ref[rows, p * LANES:(p + 1) * LANES] = (res[:, :LANES] / den).astype(BF16)
        prev = cur


def _swa(sinks, sq, skv, batch):
    n = sq.shape[0]
    steps = n // batch // SEQ_TILE
    per = SEQ_TILE // CHUNK
    return pl.pallas_call(
        _swa_kernel,
        grid=(batch, steps),
        in_specs=[
            pl.BlockSpec(memory_space=pltpu.SMEM),
            pl.BlockSpec((SEQ_TILE, SWA_Q_WIDTH), lambda b, t: (b * steps + t, 0)),
            pl.BlockSpec((SEQ_TILE, 2 * SWA_KV_WIDTH), lambda b, t: (b * steps + t, 0)),
            pl.BlockSpec((CHUNK, 2 * SWA_KV_WIDTH),
                         lambda b, t: ((b * steps + t) * per - jnp.minimum(t, 1), 0)),
        ],
        out_specs=pl.BlockSpec((SEQ_TILE, SWA_Q_WIDTH), lambda b, t: (b * steps + t, 0)),
        out_shape=jax.ShapeDtypeStruct((n, SWA_Q_WIDTH), BF16),
        compiler_params=_params2(),
        name="swa",
    )(sinks, sq, skv, skv)


def _merge_kernel(x_ref, mod_ref, ret_ref, swa_ref, gr_ref, gs_ref, wr_ref, ws_ref, wo_ref, o_ref):
    br = _wdot(ret_ref[...], wr_ref[...])
    bs = _wdot(swa_ref[...], ws_ref[...])
    merged = _sigmoid(gr_ref[...].astype(F32)) * br + _sigmoid(gs_ref[...].astype(F32)) * bs
    y = _wdot(merged.astype(BF16), wo_ref[...])
    o_ref[...] = x_ref[...] + mod_ref[5:6, :] * y


def _merge(x, mod, layer, ret, swa, gr, gs, w_ret, w_swa, w_out, batch):
    n = x.shape[0]
    tm = TOKEN_TILE
    tiles_per_batch = n // batch // tm
    rows = lambda w: pl.BlockSpec((tm, w), lambda i: (i, 0))
    return pl.pallas_call(
        _merge_kernel,
        grid=(n // tm,),
        in_specs=[
            rows(D_MODEL),
            pl.BlockSpec((None, N_MOD, D_MODEL), lambda i: (layer * batch + i // tiles_per_batch, 0, 0)),
            rows(RET_V_WIDTH), rows(SWA_Q_WIDTH), rows(D_MODEL), rows(D_MODEL),
            _layer_resident(layer, (RET_V_WIDTH, D_MODEL)), _layer_resident(layer, (SWA_Q_WIDTH, D_MODEL)),
            _layer_resident(layer, (D_MODEL, D_MODEL)),
        ],
        out_specs=rows(D_MODEL),
        out_shape=jax.ShapeDtypeStruct((n, D_MODEL), F32),
        compiler_params=_params(),
        name="merge",
    )(x, mod, ret, swa, gr, gs, w_ret, w_swa, w_out)


def kernel(x, c, positions, norm_ffn1, norm_mix, norm_ffn2, final_norm, w_ada, b_ada,
           ffn1_w_up, ffn1_w_down, ffn2_w_up, ffn2_w_down, w_in, sinks,
           w_branch_ret, w_branch_swa, w_out):
    batch, seq, d = x.shape
    n = batch * seq
    assert d == D_MODEL and seq % SEQ_TILE == 0 and seq % TOKEN_TILE == 0
    xf = x.reshape(n, d)

    pad_rows = -batch % 8
    c_pad = jnp.pad(c, ((0, pad_rows), (0, 0)))
    mod = _ada(c_pad, w_ada, b_ada)[:, :batch].reshape(DEPTH * batch, N_MOD, D_MODEL)
    rope = _rope_tables(positions)

    for l in range(DEPTH):
        xf = _ffn(xf, norm_ffn1[l], mod, l, 0, ffn1_w_up, ffn1_w_down, batch)
        rq, rk, rv, rg, sq, skv, gr, gs = _proj(xf, norm_mix[l], mod, l, rope, w_in, batch)
        ret = _retention(rq, rk, rv, rg, batch)
        swa = _swa(sinks[l], sq, skv, batch)
        xf = _merge(xf, mod, l, ret, swa, gr, gs, w_branch_ret, w_branch_swa, w_out, batch)
        final_gain = final_norm if l == DEPTH - 1 else None
        xf = _ffn(xf, norm_ffn2[l], mod, l, 6, ffn2_w_up, ffn2_w_down, batch, final_gain)
    return xf.reshape(batch, seq, d)
```

```python
import functools

import numpy as np
import jax
import jax.numpy as jnp
from jax import lax
from jax.experimental import pallas as pl
from jax.experimental.pallas import tpu as pltpu

F32 = jnp.float32
BF16 = jnp.bfloat16

D_MODEL = 1024
DEPTH = 2
RET_HEADS = 4
RET_DK = 128
RET_DV = 256
CHUNK = 128
SWA_Q_HEADS = 16
SWA_KV_HEADS = 2
SWA_HEAD_DIM = 64
D_FF = 2816
ROPE_THETA = 10000.0
NORM_EPS = 1e-6
N_MOD = 9

RET_QK_WIDTH = RET_HEADS * RET_DK
RET_V_WIDTH = RET_HEADS * RET_DV
SWA_Q_WIDTH = SWA_Q_HEADS * SWA_HEAD_DIM
SWA_KV_WIDTH = SWA_KV_HEADS * SWA_HEAD_DIM
OFF_RQ = 0
OFF_RK = OFF_RQ + RET_QK_WIDTH
OFF_RV = OFF_RK + RET_QK_WIDTH
OFF_RG = OFF_RV + RET_V_WIDTH
OFF_SQ = OFF_RG + RET_V_WIDTH
OFF_SK = OFF_SQ + SWA_Q_WIDTH
OFF_SV = OFF_SK + SWA_KV_WIDTH
OFF_GR = OFF_SV + SWA_KV_WIDTH
OFF_GS = OFF_GR + D_MODEL
IN_COLS = OFF_GS + D_MODEL

LANES = 128
ROPE_COLS = 5 * LANES

TOKEN_TILE = 512
SEQ_TILE = 512
FF_CHUNK = 256
NORM_ROWS = 32
ADA_COLS = 1152
VMEM_LIMIT = 56 * 1024 * 1024


def _sigmoid(v):
    return 1.0 / (1.0 + jnp.exp(-v))


def _dot(a, b):
    return jnp.dot(a, b, preferred_element_type=F32)


def _dot_nt(a, b):
    return lax.dot_general(a, b, (((1,), (1,)), ((), ())), preferred_element_type=F32)


def _resident(shape):
    zeros = (0,) * len(shape)
    return pl.BlockSpec(shape, lambda *_: zeros, pipeline_mode=pl.Buffered(1))


def _layer_resident(layer, shape):
    zeros = (0,) * len(shape)
    return pl.BlockSpec((None,) + tuple(shape), lambda *_: (layer,) + zeros,
                        pipeline_mode=pl.Buffered(1))


def _wdot(a, w):
    return jnp.dot(a, w.astype(BF16), preferred_element_type=F32)


def _params():
    return pltpu.CompilerParams(dimension_semantics=("arbitrary",), vmem_limit_bytes=VMEM_LIMIT)


def _params2():
    return pltpu.CompilerParams(dimension_semantics=("arbitrary", "arbitrary"),
                                vmem_limit_bytes=VMEM_LIMIT)


def _norm_mod(x, gain, shift, scale):
    y = x * lax.rsqrt(jnp.mean(x * x, axis=-1, keepdims=True) + NORM_EPS)
    return (y * gain) * (1.0 + scale) + shift


def _ada_kernel(c_ref, w_ref, b_ref, o_ref):
    c = c_ref[...]
    act = (c * _sigmoid(c)).astype(BF16)
    o_ref[...] = _dot(act, w_ref[...].astype(BF16)) + b_ref[...]


def _ada(c_pad, w_ada, b_ada):
    rows = c_pad.shape[0]
    cols = N_MOD * D_MODEL
    return pl.pallas_call(
        _ada_kernel,
        grid=(DEPTH, cols // ADA_COLS),
        in_specs=[
            pl.BlockSpec((rows, D_MODEL), lambda l, j: (0, 0)),
            pl.BlockSpec((None, D_MODEL, ADA_COLS), lambda l, j: (l, 0, j)),
            pl.BlockSpec((None, 1, ADA_COLS), lambda l, j: (l, 0, j)),
        ],
        out_specs=pl.BlockSpec((None, rows, ADA_COLS), lambda l, j: (l, 0, j)),
        out_shape=jax.ShapeDtypeStruct((DEPTH, rows, cols), F32),
        compiler_params=_params2(),
        name="ada",
    )(c_pad, w_ada, b_ada.reshape(DEPTH, 1, cols))


def _rope_kernel(pos_ref, invf_ref, o_ref):
    ang = pos_ref[...] * invf_ref[...]
    c = jnp.cos(ang)
    s = jnp.sin(ang)
    lane = lax.broadcasted_iota(jnp.int32, ang.shape, 1)
    c64 = pltpu.roll(c, 64, 1)
    s64 = pltpu.roll(s, 64, 1)
    o_ref[:, 0:LANES] = jnp.where(lane < 64, c, c64)
    o_ref[:, LANES:2 * LANES] = jnp.where(lane < 64, -s, s64)

    def tile4(v, v64):
        return jnp.where(lane < 32, v64,
                         jnp.where(lane < 64, pltpu.roll(v, 96, 1),
                                   jnp.where(lane < 96, v, pltpu.roll(v, 32, 1))))

    cs = tile4(c, c64)
    ss = tile4(s, s64)
    first_half = (lane & 63) < 32
    o_ref[:, 2 * LANES:3 * LANES] = cs
    o_ref[:, 3 * LANES:4 * LANES] = jnp.where(first_half, -ss, 0.0)
    o_ref[:, 4 * LANES:5 * LANES] = jnp.where(first_half, 0.0, ss)


def _rope_tables(positions):
    n = positions.size
    inv_r = 1.0 / (ROPE_THETA ** (jnp.arange(0, RET_DK, 2, dtype=F32) / RET_DK))
    inv_s = 1.0 / (ROPE_THETA ** (jnp.arange(0, SWA_HEAD_DIM, 2, dtype=F32) / SWA_HEAD_DIM))
    invf = jnp.concatenate([inv_r, inv_s, jnp.zeros((LANES - 96,), F32)]).reshape(1, LANES)
    pos = positions.astype(F32).reshape(n, 1)
    tm = 1024
    return pl.pallas_call(
        _rope_kernel,
        grid=(n // tm,),
        in_specs=[pl.BlockSpec((tm, 1), lambda i: (i, 0)),
                  pl.BlockSpec((1, LANES), lambda i: (0, 0))],
        out_specs=pl.BlockSpec((tm, ROPE_COLS), lambda i: (i, 0)),
        out_shape=jax.ShapeDtypeStruct((n, ROPE_COLS), F32),
        compiler_params=_params(),
        name="rope",
    )(pos, invf)


def _ffn_kernel(mod_row, final, x_ref, xn_ref, gain_ref, mod_ref, modn_ref, wup_ref, wdn_ref, *rest):
    if final:
        fgain_ref, o_ref, act_ref, h_ref = rest
    else:
        o_ref, act_ref, h_ref = rest

    def normed(xr, mr):
        return _norm_mod(xr[...], gain_ref[...], mr[mod_row:mod_row + 1, :],
                         mr[mod_row + 1:mod_row + 2, :]).astype(BF16)

    def up_chunk(j):
        lo = j * FF_CHUNK
        g = _wdot(h_ref[...], wup_ref[:, lo:lo + FF_CHUNK])
        u = _wdot(h_ref[...], wup_ref[:, D_FF + lo:D_FF + lo + FF_CHUNK])
        act_ref[:, lo:lo + FF_CHUNK] = (g * _sigmoid(g) * u).astype(BF16)

    @pl.when(pl.program_id(0) == 0)
    def _():
        h_ref[...] = normed(x_ref, mod_ref)
        up_chunk(0)

    for j in range(1, D_FF // FF_CHUNK):
        up_chunk(j)
    y = _wdot(act_ref[...], wdn_ref[...])
    h_ref[...] = normed(xn_ref, modn_ref)
    out = x_ref[...] + (0.5 * mod_ref[mod_row + 2:mod_row + 3, :]) * y
    if final:
        out = out * lax.rsqrt(jnp.mean(out * out, axis=-1, keepdims=True) + NORM_EPS)
        out = out * fgain_ref[...]
    o_ref[...] = out
    up_chunk(0)


def _tile_specs(n, tm, batch, layer):
    tiles = n // tm
    tiles_per_batch = tiles // batch
    nxt = lambda i: jnp.minimum(i + 1, tiles - 1)
    return [
        pl.BlockSpec((tm, D_MODEL), lambda i: (i, 0)),
        pl.BlockSpec((tm, D_MODEL), lambda i: (nxt(i), 0)),
        _resident((1, D_MODEL)),
        pl.BlockSpec((None, N_MOD, D_MODEL), lambda i: (layer * batch + i // tiles_per_batch, 0, 0)),
        pl.BlockSpec((None, N_MOD, D_MODEL), lambda i: (layer * batch + nxt(i) // tiles_per_batch, 0, 0)),
    ]


def _ffn(x, gain, mod, layer, mod_row, w_up, w_down, batch, final_gain=None):
    n = x.shape[0]
    tm = TOKEN_TILE
    final = final_gain is not None
    in_specs = _tile_specs(n, tm, batch, layer) + [
        _layer_resident(layer, (D_MODEL, 2 * D_FF)),
        _layer_resident(layer, (D_FF, D_MODEL)),
    ]
    args = [x, x, gain.reshape(1, D_MODEL), mod, mod, w_up, w_down]
    if final:
        in_specs.append(_resident((1, D_MODEL)))
        args.append(final_gain.reshape(1, D_MODEL))
    return pl.pallas_call(
        functools.partial(_ffn_kernel, mod_row, final),
        grid=(n // tm,),
        in_specs=in_specs,
        out_specs=pl.BlockSpec((tm, D_MODEL), lambda i: (i, 0)),
        out_shape=jax.ShapeDtypeStruct((n, D_MODEL), F32),
        scratch_shapes=[pltpu.VMEM((tm, D_FF), BF16), pltpu.VMEM((tm, D_MODEL), BF16)],
        compiler_params=_params(),
        name="ffn",
    )(*args)


def _proj_kernel(x_ref, xn_ref, gain_ref, mod_ref, modn_ref, rope_ref, w_ref,
                 rq_ref, rk_ref, rv_ref, rg_ref, sq_ref, skv_ref, gr_ref, gs_ref, h_ref, carry_ref):
    def normed(xr, mr):
        return _norm_mod(xr[...], gain_ref[...], mr[3:4, :], mr[4:5, :]).astype(BF16)

    cos_r = rope_ref[:, 0:LANES]
    sin_r = rope_ref[:, LANES:2 * LANES]
    cos_s = rope_ref[:, 2 * LANES:3 * LANES]
    sin_a = rope_ref[:, 3 * LANES:4 * LANES]
    sin_b = rope_ref[:, 4 * LANES:5 * LANES]

    def rope_ret(v):
        return v * cos_r + pltpu.roll(v, 64, 1) * sin_r

    def rope_swa(v):
        return v * cos_s + pltpu.roll(v, 96, 1) * sin_a + pltpu.roll(v, 32, 1) * sin_b

    def rope_swa_q(v):
        return rope_swa(v) * (SWA_HEAD_DIM ** -0.5)

    def project(off, width, out_ref, out_off=0, rope=None):
        y = _wdot(h_ref[...], w_ref[:, off:off + width])
        if rope is None:
            out_ref[:, out_off:out_off + width] = y.astype(out_ref.dtype)
        else:
            for s in range(width // LANES):
                slab = y[:, s * LANES:(s + 1) * LANES]
                out_ref[:, out_off + s * LANES:out_off + (s + 1) * LANES] = rope(slab).astype(out_ref.dtype)

    @pl.when(pl.program_id(0) == 0)
    def _():
        h_ref[...] = normed(x_ref, mod_ref)
        project(OFF_RV, RET_V_WIDTH, carry_ref)

    rv_ref[...] = carry_ref[...]
    project(OFF_RQ, RET_QK_WIDTH, rq_ref, rope=rope_ret)
    project(OFF_RK, RET_QK_WIDTH, rk_ref, rope=rope_ret)
    project(OFF_SQ, SWA_Q_WIDTH, sq_ref, rope=rope_swa_q)
    project(OFF_SK, SWA_KV_WIDTH, skv_ref, rope=rope_swa)
    project(OFF_SV, SWA_KV_WIDTH, skv_ref, out_off=SWA_KV_WIDTH)
    project(OFF_RG, RET_V_WIDTH, rg_ref)
    project(OFF_GR, D_MODEL, gr_ref)
    project(OFF_GS, D_MODEL, gs_ref)
    h_ref[...] = normed(xn_ref, modn_ref)
    project(OFF_RV, RET_V_WIDTH, carry_ref)


def _proj(x, gain, mod, layer, rope, w_in, batch):
    n = x.shape[0]
    tm = TOKEN_TILE
    widths = (RET_QK_WIDTH, RET_QK_WIDTH, RET_V_WIDTH, RET_V_WIDTH, SWA_Q_WIDTH,
              2 * SWA_KV_WIDTH, D_MODEL, D_MODEL)
    return pl.pallas_call(
        _proj_kernel,
        grid=(n // tm,),
        in_specs=_tile_specs(n, tm, batch, layer) + [
            pl.BlockSpec((tm, ROPE_COLS), lambda i: (i, 0)),
            _layer_resident(layer, (D_MODEL, IN_COLS)),
        ],
        out_specs=[pl.BlockSpec((tm, w), lambda i: (i, 0)) for w in widths],
        out_shape=[jax.ShapeDtypeStruct((n, w), BF16) for w in widths],
        scratch_shapes=[pltpu.VMEM((tm, D_MODEL), BF16), pltpu.VMEM((tm, RET_V_WIDTH), BF16)],
        compiler_params=_params(),
        name="proj",
    )(x, x, gain.reshape(1, D_MODEL), mod, mod, rope, w_in)


def _ret_tables():
    h = np.arange(RET_HEADS, dtype=np.float64)
    log_gamma = np.log1p(-(2.0 ** (-5.0 - h)))
    idx = np.arange(CHUNK, dtype=np.float64)
    rel = idx[:, None] - idx[None, :]
    decay = np.where(rel[None] >= 0, np.exp(np.maximum(rel, 0.0)[None] * log_gamma[:, None, None]), 0.0)
    decay = decay * (RET_DK ** -0.5)
    zeta = np.exp((CHUNK - 1 - idx)[None, :] * log_gamma[:, None])
    xi = np.exp((idx + 1)[None, :] * log_gamma[:, None]) * (RET_DK ** -0.5)
    zeta = np.broadcast_to(zeta[:, :, None], (RET_HEADS, CHUNK, RET_DK))
    xi = np.broadcast_to(xi[:, :, None], (RET_HEADS, CHUNK, RET_DK))
    gamma_chunk = tuple(float(v) for v in np.exp(CHUNK * log_gamma))
    return (jnp.asarray(decay, F32), jnp.asarray(zeta, F32), jnp.asarray(xi, F32), gamma_chunk)


def _ret_kernel(gamma_chunk, q_ref, k_ref, v_ref, g_ref, dec_ref, zeta_ref, xi_ref, o_ref, state_ref,
                raw_ref):
    @pl.when(pl.program_id(1) == 0)
    def _():
        state_ref[...] = jnp.zeros_like(state_ref)

    for c in range(SEQ_TILE // CHUNK):
        rows = slice(c * CHUNK, (c + 1) * CHUNK)
        for h in range(RET_HEADS):
            kcols = slice(h * RET_DK, (h + 1) * RET_DK)
            vcols = slice(h * RET_DV, (h + 1) * RET_DV)
            q = q_ref[rows, kcols]
            k = k_ref[rows, kcols]
            v = v_ref[rows, vcols]
            state = state_ref[h]
            scores = _dot_nt(q, k) * dec_ref[h]
            q_in = q.astype(F32) * xi_ref[h]
            kz = (k.astype(F32) * zeta_ref[h]).astype(BF16)
            lhs = jnp.concatenate(
                [jnp.concatenate([scores.astype(BF16), q_in.astype(BF16)], axis=1),
                 jnp.concatenate([kz.T, jnp.zeros((RET_DK, RET_DK), BF16)], axis=1)], axis=0)
            rhs = jnp.concatenate([v, state.astype(BF16)], axis=0)
            res = _dot(lhs, rhs)
            raw_ref[rows, vcols] = res[:CHUNK]
            state_ref[h] = state * gamma_chunk[h] + res[CHUNK:]

    for r in range(SEQ_TILE // NORM_ROWS):
        rows = slice(r * NORM_ROWS, (r + 1) * NORM_ROWS)
        for h in range(RET_HEADS):
            vcols = slice(h * RET_DV, (h + 1) * RET_DV)
            out = raw_ref[rows, vcols]
            mu = jnp.mean(out, axis=-1, keepdims=True)
            dev = out - mu
            var = jnp.mean(dev * dev, axis=-1, keepdims=True)
            gate = g_ref[rows, vcols].astype(F32)
            o_ref[rows, vcols] = (gate * _sigmoid(gate) * (dev * lax.rsqrt(var + NORM_EPS))).astype(BF16)


def _retention(rq, rk, rv, rg, batch):
    n = rq.shape[0]
    steps = n // batch // SEQ_TILE
    decay, zeta, xi, gamma_chunk = _ret_tables()
    row_block = lambda w: pl.BlockSpec((SEQ_TILE, w), lambda b, t: (b * steps + t, 0))
    return pl.pallas_call(
        functools.partial(_ret_kernel, gamma_chunk),
        grid=(batch, steps),
        in_specs=[row_block(RET_QK_WIDTH), row_block(RET_QK_WIDTH), row_block(RET_V_WIDTH),
                  row_block(RET_V_WIDTH),
                  _resident((RET_HEADS, CHUNK, CHUNK)), _resident((RET_HEADS, CHUNK, RET_DK)),
                  _resident((RET_HEADS, CHUNK, RET_DK))],
        out_specs=row_block(RET_V_WIDTH),
        out_shape=jax.ShapeDtypeStruct((n, RET_V_WIDTH), BF16),
        scratch_shapes=[pltpu.VMEM((RET_HEADS, RET_DK, RET_DV), F32),
                        pltpu.VMEM((SEQ_TILE, RET_V_WIDTH), F32)],
        compiler_params=_params2(),
        name="retention",
    )(rq, rk, rv, rg, decay, zeta, xi)


def _swa_kernel(sink_ref, q_ref, kv_ref, kvp_ref, o_ref):
    first_step = pl.program_id(1) == 0
    low = lax.broadcasted_iota(jnp.int32, (CHUNK, LANES), 1) < SWA_HEAD_DIM
    qi = lax.broadcasted_iota(jnp.int32, (CHUNK, CHUNK), 0)
    kj = lax.broadcasted_iota(jnp.int32, (CHUNK, CHUNK), 1)
    tri = kj <= qi
    zero = jnp.zeros((CHUNK, LANES), F32)
    ones_lo = jnp.where(low, 1.0, 0.0)
    ones_hi = jnp.where(low, 0.0, 1.0)

    def pieces(kvb):
        kf = kvb[:, :LANES].astype(F32)
        vf = kvb[:, LANES:].astype(F32)
        ks = pltpu.roll(kf, SWA_HEAD_DIM, 1)
        vs = pltpu.roll(vf, SWA_HEAD_DIM, 1)
        out = []
        for lo_src_k, hi_src_k, lo_src_v, hi_src_v in ((kf, ks, vf, vs), (ks, kf, vs, vf)):
            k_lo = jnp.where(low, lo_src_k, zero).astype(BF16)
            k_hi = jnp.where(low, zero, hi_src_k).astype(BF16)
            vo_lo = jnp.concatenate([jnp.where(low, lo_src_v, zero), ones_lo], axis=1).astype(BF16)
            vo_hi = jnp.concatenate([jnp.where(low, zero, hi_src_v), ones_hi], axis=1).astype(BF16)
            out.append((k_lo, k_hi, vo_lo, vo_hi))
        return out

    prev = pieces(kvp_ref[...])
    for j in range(SEQ_TILE // CHUNK):
        rows = slice(j * CHUNK, (j + 1) * CHUNK)
        cur = pieces(kv_ref[rows, :])
        kb = [jnp.concatenate([prev[g][0], cur[g][0], prev[g][1], cur[g][1]], axis=0)
              for g in range(SWA_KV_HEADS)]
        vo = [jnp.concatenate([prev[g][2], cur[g][2], prev[g][3], cur[g][3]], axis=0)
              for g in range(SWA_KV_HEADS)]
        for p in range(SWA_Q_HEADS // 2):
            g = p // (SWA_Q_HEADS // 2 // SWA_KV_HEADS)
            s = _dot_nt(q_ref[rows, p * LANES:(p + 1) * LANES], kb[g])
            probs = []
            sink_terms = []
            for hh in range(2):
                sink = sink_ref[2 * p + hh]
                s_prev = s[:, (2 * hh) * CHUNK:(2 * hh + 1) * CHUNK]
                s_cur = s[:, (2 * hh + 1) * CHUNK:(2 * hh + 2) * CHUNK]
                if j == 0:
                    s_prev = jnp.where(first_step, -jnp.inf, s_prev)
                comb = jnp.where(tri, s_cur, s_prev)
                m = jnp.maximum(jnp.max(comb, axis=-1, keepdims=True), sink)
                pe = jnp.exp(comb - m)
                p_cur = jnp.where(tri, pe, 0.0)
                probs += [pe - p_cur, p_cur]
                sink_terms.append(jnp.exp(sink - m))
            res = _dot(jnp.concatenate(probs, axis=1).astype(BF16), vo[g])
            den = res[:, LANES:] + jnp.where(low, sink_terms[0], sink_terms[1])
            o_ref[rows, p * LANES:(p + 1) * LANES] = (res[:, :LANES] / den).astype(BF16)
        prev = cur


def _swa(sinks, sq, skv, batch):
    n = sq.shape[0]
    steps = n // batch // SEQ_TILE
    per = SEQ_TILE // CHUNK
    return pl.pallas_call(
        _swa_kernel,
        grid=(batch, steps),
        in_specs=[
            pl.BlockSpec(memory_space=pltpu.SMEM),
            pl.BlockSpec((SEQ_TILE, SWA_Q_WIDTH), lambda b, t: (b * steps + t, 0)),
            pl.BlockSpec((SEQ_TILE, 2 * SWA_KV_WIDTH), lambda b, t: (b * steps + t, 0)),
            pl.BlockSpec((CHUNK, 2 * SWA_KV_WIDTH),
                         lambda b, t: ((b * steps + t) * per - jnp.minimum(t, 1), 0)),
        ],
        out_specs=pl.BlockSpec((SEQ_TILE, SWA_Q_WIDTH), lambda b, t: (b * steps + t, 0)),
        out_shape=jax.ShapeDtypeStruct((n, SWA_Q_WIDTH), BF16),
        compiler_params=_params2(),
        name="swa",
    )(sinks, sq, skv, skv)


def _merge_kernel(x_ref, mod_ref, ret_ref, swa_ref, gr_ref, gs_ref, wr_ref, ws_ref, wo_ref, o_ref):
    br = _wdot(ret_ref[...], wr_ref[...])
    bs = _wdot(swa_ref[...], ws_ref[...])
    merged = _sigmoid(gr_ref[...].astype(F32)) * br + _sigmoid(gs_ref[...].astype(F32)) * bs
    y = _wdot(merged.astype(BF16), wo_ref[...])
    o_ref[...] = x_ref[...] + mod_ref[5:6, :] * y


def _merge(x, mod, layer, ret, swa, gr, gs, w_ret, w_swa, w_out, batch):
    n = x.shape[0]
    tm = TOKEN_TILE
    tiles_per_batch = n // batch // tm
    rows = lambda w: pl.BlockSpec((tm, w), lambda i: (i, 0))
    return pl.pallas_call(
        _merge_kernel,
        grid=(n // tm,),
        in_specs=[
            rows(D_MODEL),
            pl.BlockSpec((None, N_MOD, D_MODEL), lambda i: (layer * batch + i // tiles_per_batch, 0, 0)),
            rows(RET_V_WIDTH), rows(SWA_Q_WIDTH), rows(D_MODEL), rows(D_MODEL),
            _layer_resident(layer, (RET_V_WIDTH, D_MODEL)), _layer_resident(layer, (SWA_Q_WIDTH, D_MODEL)),
            _layer_resident(layer, (D_MODEL, D_MODEL)),
        ],
        out_specs=rows(D_MODEL),
        out_shape=jax.ShapeDtypeStruct((n, D_MODEL), F32),
        compiler_params=_params(),
        name="merge",
    )(x, mod, ret, swa, gr, gs, w_ret, w_swa, w_out)


def kernel(x, c, positions, norm_ffn1, norm_mix, norm_ffn2, final_norm, w_ada, b_ada,
           ffn1_w_up, ffn1_w_down, ffn2_w_up, ffn2_w_down, w_in, sinks,
           w_branch_ret, w_branch_swa, w_out):
    batch, seq, d = x.shape
    n = batch * seq
    assert d == D_MODEL and seq % SEQ_TILE == 0 and seq % TOKEN_TILE == 0
    xf = x.reshape(n, d)

    pad_rows = -batch % 8
    c_pad = jnp.pad(c, ((0, pad_rows), (0, 0)))
    mod = _ada(c_pad, w_ada, b_ada)[:, :batch].reshape(DEPTH * batch, N_MOD, D_MODEL)
    rope = _rope_tables(positions)

    for l in range(DEPTH):
        xf = _ffn(xf, norm_ffn1[l], mod, l, 0, ffn1_w_up, ffn1_w_down, batch)
        rq, rk, rv, rg, sq, skv, gr, gs = _proj(xf, norm_mix[l], mod, l, rope, w_in, batch)
        ret = _retention(rq, rk, rv, rg, batch)
        swa = _swa(sinks[l], sq, skv, batch)
        xf = _merge(xf, mod, l, ret, swa, gr, gs, w_branch_ret, w_branch_swa, w_out, batch)
        final_gain = final_norm if l == DEPTH - 1 else None
        xf = _ffn(xf, norm_ffn2[l], mod, l, 6, ffn2_w_up, ffn2_w_down, batch, final_gain)
    return xf.reshape(batch, seq, d)
```

```python
import functools

import numpy as np
import jax
import jax.numpy as jnp
from jax import lax
from jax.experimental import pallas as pl
from jax.experimental.pallas import tpu as pltpu

F32 = jnp.float32
BF16 = jnp.bfloat16

D_MODEL = 1024
DEPTH = 2
RET_HEADS = 4
RET_DK = 128
RET_DV = 256
CHUNK = 128
SWA_Q_HEADS = 16
SWA_KV_HEADS = 2
SWA_HEAD_DIM = 64
D_FF = 2816
ROPE_THETA = 10000.0
NORM_EPS = 1e-6
N_MOD = 9

RET_QK_WIDTH = RET_HEADS * RET_DK
RET_V_WIDTH = RET_HEADS * RET_DV
SWA_Q_WIDTH = SWA_Q_HEADS * SWA_HEAD_DIM
SWA_KV_WIDTH = SWA_KV_HEADS * SWA_HEAD_DIM
OFF_RQ = 0
OFF_RK = OFF_RQ + RET_QK_WIDTH
OFF_RV = OFF_RK + RET_QK_WIDTH
OFF_RG = OFF_RV + RET_V_WIDTH
OFF_SQ = OFF_RG + RET_V_WIDTH
OFF_SK = OFF_SQ + SWA_Q_WIDTH
OFF_SV = OFF_SK + SWA_KV_WIDTH
OFF_GR = OFF_SV + SWA_KV_WIDTH
OFF_GS = OFF_GR + D_MODEL
IN_COLS = OFF_GS + D_MODEL

LANES = 128
ROPE_COLS = 5 * LANES

TOKEN_TILE = 512
SEQ_TILE = 512
FF_CHUNK = 256
FFN_DOWN_PARTS = 4
PROJ_SEGMENTS = 8
NORM_ROWS = 32
RET_BLOCK = 256
ADA_COLS = 1152
VMEM_LIMIT = 56 * 1024 * 1024


def _sigmoid(v):
    return 1.0 / (1.0 + jnp.exp(-v))


def _dot(a, b):
    return jnp.dot(a, b, preferred_element_type=F32)


def _dot_nt(a, b):
    return lax.dot_general(a, b, (((1,), (1,)), ((), ())), preferred_element_type=F32)


def _resident(shape):
    zeros = (0,) * len(shape)
    return pl.BlockSpec(shape, lambda *_: zeros, pipeline_mode=pl.Buffered(1))


def _layer_resident(layer, shape):
    zeros = (0,) * len(shape)
    return pl.BlockSpec((None,) + tuple(shape), lambda *_: (layer,) + zeros,
                        pipeline_mode=pl.Buffered(1))


def _wdot(a, w):
    return jnp.dot(a, w.astype(BF16), preferred_element_type=F32)


def _params():
    return pltpu.CompilerParams(dimension_semantics=("arbitrary",), vmem_limit_bytes=VMEM_LIMIT)


def _params2():
    return pltpu.CompilerParams(dimension_semantics=("arbitrary", "arbitrary"),
                                vmem_limit_bytes=VMEM_LIMIT)


def _norm_mod(x, gain, shift, scale):
    y = x * lax.rsqrt(jnp.mean(x * x, axis=-1, keepdims=True) + NORM_EPS)
    return (y * gain) * (1.0 + scale) + shift


def _ada_kernel(c_ref, w_ref, b_ref, o_ref):
    c = c_ref[...]
    act = (c * _sigmoid(c)).astype(BF16)
    o_ref[...] = _dot(act, w_ref[...].astype(BF16)) + b_ref[...]


def _rope_kernel(pos_ref, invf_ref, o_ref):
    ang = pos_ref[...] * invf_ref[...]
    c = jnp.cos(ang)
    s = jnp.sin(ang)
    lane = lax.broadcasted_iota(jnp.int32, ang.shape, 1)
    c64 = pltpu.roll(c, 64, 1)
    s64 = pltpu.roll(s, 64, 1)
    o_ref[:, 0:LANES] = jnp.where(lane < 64, c, c64)
    o_ref[:, LANES:2 * LANES] = jnp.where(lane < 64, -s, s64)

    def tile4(v, v64):
        return jnp.where(lane < 32, v64,
                         jnp.where(lane < 64, pltpu.roll(v, 96, 1),
                                   jnp.where(lane < 96, v, pltpu.roll(v, 32, 1))))

    cs = tile4(c, c64)
    ss = tile4(s, s64)
    first_half = (lane & 63) < 32
    o_ref[:, 2 * LANES:3 * LANES] = cs
    o_ref[:, 3 * LANES:4 * LANES] = jnp.where(first_half, -ss, 0.0)
    o_ref[:, 4 * LANES:5 * LANES] = jnp.where(first_half, 0.0, ss)


def _prep_kernel(c_ref, w_ref, b_ref, pos_ref, invf_ref, mod_ref, rope_ref):
    _ada_kernel(c_ref, w_ref, b_ref, mod_ref)
    _rope_kernel(pos_ref, invf_ref, rope_ref)


def _prep(c_pad, w_ada, b_ada, positions):
    rows = c_pad.shape[0]
    cols = N_MOD * D_MODEL
    blocks = cols // ADA_COLS
    steps = DEPTH * blocks
    n = positions.size
    assert n % steps == 0
    tm = n // steps
    inv_r = 1.0 / (ROPE_THETA ** (jnp.arange(0, RET_DK, 2, dtype=F32) / RET_DK))
    inv_s = 1.0 / (ROPE_THETA ** (jnp.arange(0, SWA_HEAD_DIM, 2, dtype=F32) / SWA_HEAD_DIM))
    invf = jnp.concatenate([inv_r, inv_s, jnp.zeros((LANES - 96,), F32)]).reshape(1, LANES)
    pos = positions.astype(F32).reshape(n, 1)
    return pl.pallas_call(
        _prep_kernel,
        grid=(steps,),
        in_specs=[
            pl.BlockSpec((rows, D_MODEL), lambda i: (0, 0)),
            pl.BlockSpec((None, D_MODEL, ADA_COLS), lambda i: (i // blocks, 0, i % blocks)),
            pl.BlockSpec((None, 1, ADA_COLS), lambda i: (i // blocks, 0, i % blocks)),
            pl.BlockSpec((tm, 1), lambda i: (i, 0)),
            pl.BlockSpec((1, LANES), lambda i: (0, 0)),
        ],
        out_specs=[pl.BlockSpec((None, rows, ADA_COLS), lambda i: (i // blocks, 0, i % blocks)),
                   pl.BlockSpec((tm, ROPE_COLS), lambda i: (i, 0))],
        out_shape=[jax.ShapeDtypeStruct((DEPTH, rows, cols), F32),
                   jax.ShapeDtypeStruct((n, ROPE_COLS), F32)],
        compiler_params=_params(),
        name="prep",
    )(c_pad, w_ada, b_ada.reshape(DEPTH, 1, cols), pos, invf)


def _ffn_kernel(layer, mod_row, final, x_ref, xn_ref, gain_ref, mod_ref, modn_ref, wup_hbm, wdn_hbm, *rest):
    if final:
        fgain_ref, o_ref, act_ref, h_ref, wup_ref, wdn_ref, sem = rest
    else:
        o_ref, act_ref, h_ref, wup_ref, wdn_ref, sem = rest
    n_up = D_FF // FF_CHUNK
    dn_rows = D_FF // FFN_DOWN_PARTS

    def up_copy(j, half):
        cols = pl.ds(half * D_FF + j * FF_CHUNK, FF_CHUNK)
        return pltpu.make_async_copy(wup_hbm.at[layer, :, cols], wup_ref.at[:, cols],
                                     sem.at[2 * j + half])

    def down_copy(p):
        rows = pl.ds(p * dn_rows, dn_rows)
        return pltpu.make_async_copy(wdn_hbm.at[layer, rows, :], wdn_ref.at[rows, :],
                                     sem.at[2 * n_up + p])

    def normed(xr, mr):
        return _norm_mod(xr[...], gain_ref[...], mr[mod_row:mod_row + 1, :],
                         mr[mod_row + 1:mod_row + 2, :]).astype(BF16)

    def up_chunk(j):
        lo = j * FF_CHUNK
        g = _wdot(h_ref[...], wup_ref[:, lo:lo + FF_CHUNK])
        u = _wdot(h_ref[...], wup_ref[:, D_FF + lo:D_FF + lo + FF_CHUNK])
        act_ref[:, lo:lo + FF_CHUNK] = (g * _sigmoid(g) * u).astype(BF16)

    def finish_tile():
        y = _wdot(act_ref[...], wdn_ref[...])
        h_ref[...] = normed(xn_ref, modn_ref)
        out = x_ref[...] + (0.5 * mod_ref[mod_row + 2:mod_row + 3, :]) * y
        if final:
            out = out * lax.rsqrt(jnp.mean(out * out, axis=-1, keepdims=True) + NORM_EPS)
            out = out * fgain_ref[...]
        o_ref[...] = out
        up_chunk(0)

    @pl.when(pl.program_id(0) == 0)
    def _():
        for j in range(n_up):
            up_copy(j, 0).start()
            up_copy(j, 1).start()
        for p in range(FFN_DOWN_PARTS):
            down_copy(p).start()
        h_ref[...] = normed(x_ref, mod_ref)
        for j in range(n_up):
            up_copy(j, 0).wait()
            up_copy(j, 1).wait()
            up_chunk(j)
        for p in range(FFN_DOWN_PARTS):
            down_copy(p).wait()
        finish_tile()

    @pl.when(pl.program_id(0) > 0)
    def _():
        for j in range(1, n_up):
            up_chunk(j)
        finish_tile()


def _tile_specs(n, tm, batch, layer):
    tiles = n // tm
    tiles_per_batch = tiles // batch
    nxt = lambda i: jnp.minimum(i + 1, tiles - 1)
    return [
        pl.BlockSpec((tm, D_MODEL), lambda i: (i, 0)),
        pl.BlockSpec((tm, D_MODEL), lambda i: (nxt(i), 0)),
        _resident((1, D_MODEL)),
        pl.BlockSpec((None, N_MOD, D_MODEL), lambda i: (layer * batch + i // tiles_per_batch, 0, 0)),
        pl.BlockSpec((None, N_MOD, D_MODEL), lambda i: (layer * batch + nxt(i) // tiles_per_batch, 0, 0)),
    ]


def _ffn(x, gain, mod, layer, mod_row, w_up, w_down, batch, final_gain=None):
    n = x.shape[0]
    tm = TOKEN_TILE
    final = final_gain is not None
    in_specs = _tile_specs(n, tm, batch, layer) + [
        pl.BlockSpec(memory_space=pl.ANY),
        pl.BlockSpec(memory_space=pl.ANY),
    ]
    args = [x, x, gain.reshape(1, D_MODEL), mod, mod, w_up, w_down]
    if final:
        in_specs.append(_resident((1, D_MODEL)))
        args.append(final_gain.reshape(1, D_MODEL))
    return pl.pallas_call(
        functools.partial(_ffn_kernel, layer, mod_row, final),
        grid=(n // tm,),
        in_specs=in_specs,
        out_specs=pl.BlockSpec((tm, D_MODEL), lambda i: (i, 0)),
        out_shape=jax.ShapeDtypeStruct((n, D_MODEL), F32),
        scratch_shapes=[pltpu.VMEM((tm, D_FF), BF16), pltpu.VMEM((tm, D_MODEL), BF16),
                        pltpu.VMEM((D_MODEL, 2 * D_FF), F32), pltpu.VMEM((D_FF, D_MODEL), F32),
                        pltpu.SemaphoreType.DMA((2 * (D_FF // FF_CHUNK) + FFN_DOWN_PARTS,))],
        compiler_params=_params(),
        name="ffn",
    )(*args)


def _proj_kernel(layer, x_ref, xn_ref, gain_ref, mod_ref, modn_ref, rope_ref, w_hbm,
                 rq_ref, rk_ref, rv_ref, rg_ref, sq_ref, skv_ref, gr_ref, gs_ref,
                 h_ref, carry_ref, w_ref, sem):
    def normed(xr, mr):
        return _norm_mod(xr[...], gain_ref[...], mr[3:4, :], mr[4:5, :]).astype(BF16)

    cos_r = rope_ref[:, 0:LANES]
    sin_r = rope_ref[:, LANES:2 * LANES]
    cos_s = rope_ref[:, 2 * LANES:3 * LANES]
    sin_a = rope_ref[:, 3 * LANES:4 * LANES]
    sin_b = rope_ref[:, 4 * LANES:5 * LANES]

    def rope_ret(v):
        return v * cos_r + pltpu.roll(v, 64, 1) * sin_r

    def rope_swa(v):
        return v * cos_s + pltpu.roll(v, 96, 1) * sin_a + pltpu.roll(v, 32, 1) * sin_b

    def rope_swa_q(v):
        return rope_swa(v) * (SWA_HEAD_DIM ** -0.5)

    def project(off, width, out_ref, out_off=0, rope=None):
        y = _wdot(h_ref[...], w_ref[:, off:off + width])
        if rope is None:
            out_ref[:, out_off:out_off + width] = y.astype(out_ref.dtype)
        else:
            for s in range(width // LANES):
                slab = y[:, s * LANES:(s + 1) * LANES]
                out_ref[:, out_off + s * LANES:out_off + (s + 1) * LANES] = rope(slab).astype(out_ref.dtype)

    def seg_skv():
        project(OFF_SK, SWA_KV_WIDTH, skv_ref, rope=rope_swa)
        project(OFF_SV, SWA_KV_WIDTH, skv_ref, out_off=SWA_KV_WIDTH)

    segments = (
        (OFF_RV, RET_V_WIDTH, None),
        (OFF_RQ, RET_QK_WIDTH, lambda: project(OFF_RQ, RET_QK_WIDTH, rq_ref, rope=rope_ret)),
        (OFF_RK, RET_QK_WIDTH, lambda: project(OFF_RK, RET_QK_WIDTH, rk_ref, rope=rope_ret)),
        (OFF_SQ, SWA_Q_WIDTH, lambda: project(OFF_SQ, SWA_Q_WIDTH, sq_ref, rope=rope_swa_q)),
        (OFF_SK, 2 * SWA_KV_WIDTH, seg_skv),
        (OFF_RG, RET_V_WIDTH, lambda: project(OFF_RG, RET_V_WIDTH, rg_ref)),
        (OFF_GR, D_MODEL, lambda: project(OFF_GR, D_MODEL, gr_ref)),
        (OFF_GS, D_MODEL, lambda: project(OFF_GS, D_MODEL, gs_ref)),
    )

    assert len(segments) == PROJ_SEGMENTS

    def w_copy(s):
        cols = pl.ds(segments[s][0], segments[s][1])
        return pltpu.make_async_copy(w_hbm.at[layer, :, cols], w_ref.at[:, cols], sem.at[s])

    def start_next_tile():
        h_ref[...] = normed(xn_ref, modn_ref)
        project(OFF_RV, RET_V_WIDTH, carry_ref)

    @pl.when(pl.program_id(0) == 0)
    def _():
        for s in range(len(segments)):
            w_copy(s).start()
        h_ref[...] = normed(x_ref, mod_ref)
        w_copy(0).wait()
        project(OFF_RV, RET_V_WIDTH, carry_ref)
        rv_ref[...] = carry_ref[...]
        for s in range(1, len(segments)):
            w_copy(s).wait()
            segments[s][2]()
        start_next_tile()

    @pl.when(pl.program_id(0) > 0)
    def _():
        rv_ref[...] = carry_ref[...]
        for s in range(1, len(segments)):
            segments[s][2]()
        start_next_tile()


def _proj(x, gain, mod, layer, rope, w_in, batch):
    n = x.shape[0]
    tm = TOKEN_TILE
    widths = (RET_QK_WIDTH, RET_QK_WIDTH, RET_V_WIDTH, RET_V_WIDTH, SWA_Q_WIDTH,
              2 * SWA_KV_WIDTH, D_MODEL, D_MODEL)
    return pl.pallas_call(
        functools.partial(_proj_kernel, layer),
        grid=(n // tm,),
        in_specs=_tile_specs(n, tm, batch, layer) + [
            pl.BlockSpec((tm, ROPE_COLS), lambda i: (i, 0)),
            pl.BlockSpec(memory_space=pl.ANY),
        ],
        out_specs=[pl.BlockSpec((tm, w), lambda i: (i, 0)) for w in widths],
        out_shape=[jax.ShapeDtypeStruct((n, w), BF16) for w in widths],
        scratch_shapes=[pltpu.VMEM((tm, D_MODEL), BF16), pltpu.VMEM((tm, RET_V_WIDTH), BF16),
                        pltpu.VMEM((D_MODEL, IN_COLS), F32), pltpu.SemaphoreType.DMA((PROJ_SEGMENTS,))],
        compiler_params=_params(),
        name="proj",
    )(x, x, gain.reshape(1, D_MODEL), mod, mod, rope, w_in)


def _ret_tables():
    h = np.arange(RET_HEADS, dtype=np.float64)
    log_gamma = np.log1p(-(2.0 ** (-5.0 - h)))
    idx = np.arange(RET_BLOCK, dtype=np.float64)
    rel = idx[:, None] - idx[None, :]
    decay = np.where(rel[None] >= 0, np.exp(np.maximum(rel, 0.0)[None] * log_gamma[:, None, None]), 0.0)
    decay = decay * (RET_DK ** -0.5)
    zeta = np.exp((RET_BLOCK - 1 - idx)[None, :] * log_gamma[:, None])
    xi = np.exp((idx + 1)[None, :] * log_gamma[:, None]) * (RET_DK ** -0.5)
    zeta = np.broadcast_to(zeta[:, :, None], (RET_HEADS, RET_BLOCK, RET_DK))
    xi = np.broadcast_to(xi[:, :, None], (RET_HEADS, RET_BLOCK, RET_DK))
    gamma_chunk = tuple(float(v) for v in np.exp(RET_BLOCK * log_gamma))
    return (jnp.asarray(decay, F32), jnp.asarray(zeta, F32), jnp.asarray(xi, F32), gamma_chunk)


def _ret_kernel(gamma_chunk, q_ref, k_ref, v_ref, g_ref, dec_ref, zeta_ref, xi_ref, o_ref, state_ref,
                raw_ref):
    @pl.when(pl.program_id(1) == 0)
    def _():
        state_ref[...] = jnp.zeros_like(state_ref)

    for c in range(SEQ_TILE // RET_BLOCK):
        rows = slice(c * RET_BLOCK, (c + 1) * RET_BLOCK)
        for h in range(RET_HEADS):
            kcols = slice(h * RET_DK, (h + 1) * RET_DK)
            vcols = slice(h * RET_DV, (h + 1) * RET_DV)
            q = q_ref[rows, kcols]
            k = k_ref[rows, kcols]
            v = v_ref[rows, vcols]
            state = state_ref[h]
            scores = _dot_nt(q, k) * dec_ref[h]
            q_in = q.astype(F32) * xi_ref[h]
            kz = (k.astype(F32) * zeta_ref[h]).astype(BF16)
            lhs = jnp.concatenate(
                [jnp.concatenate([scores.astype(BF16), q_in.astype(BF16)], axis=1),
                 jnp.concatenate([kz.T, jnp.zeros((RET_DK, RET_DK), BF16)], axis=1)], axis=0)
            rhs = jnp.concatenate([v, state.astype(BF16)], axis=0)
            res = _dot(lhs, rhs)
            raw_ref[rows, vcols] = res[:RET_BLOCK]
            state_ref[h] = state * gamma_chunk[h] + res[RET_BLOCK:]

    for r in range(SEQ_TILE // NORM_ROWS):
        rows = slice(r * NORM_ROWS, (r + 1) * NORM_ROWS)
        for h in range(RET_HEADS):
            vcols = slice(h * RET_DV, (h + 1) * RET_DV)
            out = raw_ref[rows, vcols]
            mu = jnp.mean(out, axis=-1, keepdims=True)
            dev = out - mu
            var = jnp.mean(dev * dev, axis=-1, keepdims=True)
            gate = g_ref[rows, vcols].astype(F32)
            o_ref[rows, vcols] = (gate * _sigmoid(gate) * (dev * lax.rsqrt(var + NORM_EPS))).astype(BF16)


def _retention(rq, rk, rv, rg, batch):
    n = rq.shape[0]
    steps = n // batch // SEQ_TILE
    decay, zeta, xi, gamma_chunk = _ret_tables()
    row_block = lambda w: pl.BlockSpec((SEQ_TILE, w), lambda b, t: (b * steps + t, 0))
    return pl.pallas_call(
        functools.partial(_ret_kernel, gamma_chunk),
        grid=(batch, steps),
        in_specs=[row_block(RET_QK_WIDTH), row_block(RET_QK_WIDTH), row_block(RET_V_WIDTH),
                  row_block(RET_V_WIDTH),
                  _resident((RET_HEADS, RET_BLOCK, RET_BLOCK)), _resident((RET_HEADS, RET_BLOCK, RET_DK)),
                  _resident((RET_HEADS, RET_BLOCK, RET_DK))],
        out_specs=row_block(RET_V_WIDTH),
        out_shape=jax.ShapeDtypeStruct((n, RET_V_WIDTH), BF16),
        scratch_shapes=[pltpu.VMEM((RET_HEADS, RET_DK, RET_DV), F32),
                        pltpu.VMEM((SEQ_TILE, RET_V_WIDTH), F32)],
        compiler_params=_params2(),
        name="retention",
    )(rq, rk, rv, rg, decay, zeta, xi)


def _swa_kernel(sink_ref, q_ref, kv_ref, kvp_ref, o_ref):
    first_step = pl.program_id(1) == 0
    low = lax.broadcasted_iota(jnp.int32, (CHUNK, LANES), 1) < SWA_HEAD_DIM
    qi = lax.broadcasted_iota(jnp.int32, (CHUNK, CHUNK), 0)
    kj = lax.broadcasted_iota(jnp.int32, (CHUNK, CHUNK), 1)
    tri = kj <= qi
    zero = jnp.zeros((CHUNK, LANES), F32)
    ones_lo = jnp.where(low, 1.0, 0.0)
    ones_hi = jnp.where(low, 0.0, 1.0)

    def pieces(kvb):
        kf = kvb[:, :LANES].astype(F32)
        vf = kvb[:, LANES:].astype(F32)
        ks = pltpu.roll(kf, SWA_HEAD_DIM, 1)
        vs = pltpu.roll(vf, SWA_HEAD_DIM, 1)
        out = []
        for lo_src_k, hi_src_k, lo_src_v, hi_src_v in ((kf, ks, vf, vs), (ks, kf, vs, vf)):
            k_lo = jnp.where(low, lo_src_k, zero).astype(BF16)
            k_hi = jnp.where(low, zero, hi_src_k).astype(BF16)
            vo_lo = jnp.concatenate([jnp.where(low, lo_src_v, zero), ones_lo], axis=1).astype(BF16)
            vo_hi = jnp.concatenate([jnp.where(low, zero, hi_src_v), ones_hi], axis=1).astype(BF16)
            out.append((k_lo, k_hi, vo_lo, vo_hi))
        return out

    prev = pieces(kvp_ref[...])
    for j in range(SEQ_TILE // CHUNK):
        rows = slice(j * CHUNK, (j + 1) * CHUNK)
        cur = pieces(kv_ref[rows, :])
        kb = [jnp.concatenate([prev[g][0], cur[g][0], prev[g][1], cur[g][1]], axis=0)
              for g in range(SWA_KV_HEADS)]
        vo = [jnp.concatenate([prev[g][2], cur[g][2], prev[g][3], cur[g][3]], axis=0)
              for g in range(SWA_KV_HEADS)]
        for p in range(SWA_Q_HEADS // 2):
            g = p // (SWA_Q_HEADS // 2 // SWA_KV_HEADS)
            s = _dot_nt(q_ref[rows, p * LANES:(p + 1) * LANES], kb[g])
            probs = []
            sink_terms = []
            for hh in range(2):
                sink = sink_ref[2 * p + hh]
                s_prev = s[:, (2 * hh) * CHUNK:(2 * hh + 1) * CHUNK]
                s_cur = s[:, (2 * hh + 1) * CHUNK:(2 * hh + 2) * CHUNK]
                if j == 0:
                    s_prev = jnp.where(first_step, -jnp.inf, s_prev)
                comb = jnp.where(tri, s_cur, s_prev)
                m = jnp.maximum(jnp.max(comb, axis=-1, keepdims=True), sink)
                pe = jnp.exp(comb - m)
                p_cur = jnp.where(tri, pe, 0.0)
                probs += [pe - p_cur, p_cur]
                sink_terms.append(jnp.exp(sink - m))
            res = _dot(jnp.concatenate(probs, axis=1).astype(BF16), vo[g])
            den = res[:, LANES:] + jnp.where(low, sink_terms[0], sink_terms[1])
            o_ref[rows, p * LANES:(p + 1) * LANES] = (res[:, :LANES] / den).astype(BF16)
        prev = cur


def _swa(sinks, sq, skv, batch):
    n = sq.shape[0]
    steps = n // batch // SEQ_TILE
    per = SEQ_TILE // CHUNK
    return pl.pallas_call(
        _swa_kernel,
        grid=(batch, steps),
        in_specs=[
            pl.BlockSpec(memory_space=pltpu.SMEM),
            pl.BlockSpec((SEQ_TILE, SWA_Q_WIDTH), lambda b, t: (b * steps + t, 0)),
            pl.BlockSpec((SEQ_TILE, 2 * SWA_KV_WIDTH), lambda b, t: (b * steps + t, 0)),
            pl.BlockSpec((CHUNK, 2 * SWA_KV_WIDTH),
                         lambda b, t: ((b * steps + t) * per - jnp.minimum(t, 1), 0)),
        ],
        out_specs=pl.BlockSpec((SEQ_TILE, SWA_Q_WIDTH), lambda b, t: (b * steps + t, 0)),
        out_shape=jax.ShapeDtypeStruct((n, SWA_Q_WIDTH), BF16),
        compiler_params=_params2(),
        name="swa",
    )(sinks, sq, skv, skv)


def _merge_kernel(x_ref, mod_ref, ret_ref, swa_ref, gr_ref, gs_ref, wr_ref, ws_ref, wo_ref, o_ref):
    br = _wdot(ret_ref[...], wr_ref[...])
    bs = _wdot(swa_ref[...], ws_ref[...])
    merged = _sigmoid(gr_ref[...].astype(F32)) * br + _sigmoid(gs_ref[...].astype(F32)) * bs
    y = _wdot(merged.astype(BF16), wo_ref[...])
    o_ref[...] = x_ref[...] + mod_ref[5:6, :] * y


def _merge(x, mod, layer, ret, swa, gr, gs, w_ret, w_swa, w_out, batch):
    n = x.shape[0]
    tm = TOKEN_TILE
    tiles_per_batch = n // batch // tm
    rows = lambda w: pl.BlockSpec((tm, w), lambda i: (i, 0))
    return pl.pallas_call(
        _merge_kernel,
        grid=(n // tm,),
        in_specs=[
            rows(D_MODEL),
            pl.BlockSpec((None, N_MOD, D_MODEL), lambda i: (layer * batch + i // tiles_per_batch, 0, 0)),
            rows(RET_V_WIDTH), rows(SWA_Q_WIDTH), rows(D_MODEL), rows(D_MODEL),
            _layer_resident(layer, (RET_V_WIDTH, D_MODEL)), _layer_resident(layer, (SWA_Q_WIDTH, D_MODEL)),
            _layer_resident(layer, (D_MODEL, D_MODEL)),
        ],
        out_specs=rows(D_MODEL),
        out_shape=jax.ShapeDtypeStruct((n, D_MODEL), F32),
        compiler_params=_params(),
        name="merge",
    )(x, mod, ret, swa, gr, gs, w_ret, w_swa, w_out)


def kernel(x, c, positions, norm_ffn1, norm_mix, norm_ffn2, final_norm, w_ada, b_ada,
           ffn1_w_up, ffn1_w_down, ffn2_w_up, ffn2_w_down, w_in, sinks,
           w_branch_ret, w_branch_swa, w_out):
    batch, seq, d = x.shape
    n = batch * seq
    assert d == D_MODEL and seq % SEQ_TILE == 0 and seq % TOKEN_TILE == 0
    xf = x.reshape(n, d)

    pad_rows = -batch % 8
    c_pad = jnp.pad(c, ((0, pad_rows), (0, 0)))
    mod, rope = _prep(c_pad, w_ada, b_ada, positions)
    mod = mod[:, :batch].reshape(DEPTH * batch, N_MOD, D_MODEL)

    for l in range(DEPTH):
        xf = _ffn(xf, norm_ffn1[l], mod, l, 0, ffn1_w_up, ffn1_w_down, batch)
        rq, rk, rv, rg, sq, skv, gr, gs = _proj(xf, norm_mix[l], mod, l, rope, w_in, batch)
        ret = _retention(rq, rk, rv, rg, batch)
        swa = _swa(sinks[l], sq, skv, batch)
        xf = _merge(xf, mod, l, ret, swa, gr, gs, w_branch_ret, w_branch_swa, w_out, batch)
        final_gain = final_norm if l == DEPTH - 1 else None
        xf = _ffn(xf, norm_ffn2[l], mod, l, 6, ffn2_w_up, ffn2_w_down, batch, final_gain)
    return xf.reshape(batch, seq, d)
```

```python
import functools

import numpy as np
import jax
import jax.numpy as jnp
from jax import lax
from jax.experimental import pallas as pl
from jax.experimental.pallas import tpu as pltpu

F32 = jnp.float32
BF16 = jnp.bfloat16

D_MODEL = 1024
DEPTH = 2
RET_HEADS = 4
RET_DK = 128
RET_DV = 256
CHUNK = 128
SWA_Q_HEADS = 16
SWA_KV_HEADS = 2
SWA_HEAD_DIM = 64
D_FF = 2816
ROPE_THETA = 10000.0
NORM_EPS = 1e-6
N_MOD = 9

RET_QK_WIDTH = RET_HEADS * RET_DK
RET_V_WIDTH = RET_HEADS * RET_DV
SWA_Q_WIDTH = SWA_Q_HEADS * SWA_HEAD_DIM
SWA_KV_WIDTH = SWA_KV_HEADS * SWA_HEAD_DIM
OFF_RQ = 0
OFF_RK = OFF_RQ + RET_QK_WIDTH
OFF_RV = OFF_RK + RET_QK_WIDTH
OFF_RG = OFF_RV + RET_V_WIDTH
OFF_SQ = OFF_RG + RET_V_WIDTH
OFF_SK = OFF_SQ + SWA_Q_WIDTH
OFF_SV = OFF_SK + SWA_KV_WIDTH
OFF_GR = OFF_SV + SWA_KV_WIDTH
OFF_GS = OFF_GR + D_MODEL
IN_COLS = OFF_GS + D_MODEL

LANES = 128
ROPE_COLS = 5 * LANES

TOKEN_TILE = 512
MERGE_TILE = 1024
SEQ_TILE = 1024
FF_CHUNK = 256
FFN_DOWN_PARTS = 4
PROJ_SEGMENTS = 8
NORM_ROWS = 32
RET_BLOCK = 256
ADA_COLS = 1152
VMEM_LIMIT = 56 * 1024 * 1024


def _sigmoid(v):
    return 1.0 / (1.0 + jnp.exp(-v))


def _dot(a, b):
    return jnp.dot(a, b, preferred_element_type=F32)


def _dot_nt(a, b):
    return lax.dot_general(a, b, (((1,), (1,)), ((), ())), preferred_element_type=F32)


def _resident(shape):
    zeros = (0,) * len(shape)
    return pl.BlockSpec(shape, lambda *_: zeros, pipeline_mode=pl.Buffered(1))


def _layer_resident(layer, shape):
    zeros = (0,) * len(shape)
    return pl.BlockSpec((None,) + tuple(shape), lambda *_: (layer,) + zeros,
                        pipeline_mode=pl.Buffered(1))


def _wdot(a, w):
    return jnp.dot(a, w.astype(BF16), preferred_element_type=F32)


def _params():
    return pltpu.CompilerParams(dimension_semantics=("arbitrary",), vmem_limit_bytes=VMEM_LIMIT)


def _params2():
    return pltpu.CompilerParams(dimension_semantics=("arbitrary", "arbitrary"),
                                vmem_limit_bytes=VMEM_LIMIT)


def _norm_mod(x, gain, shift, scale):
    y = x * lax.rsqrt(jnp.mean(x * x, axis=-1, keepdims=True) + NORM_EPS)
    return (y * gain) * (1.0 + scale) + shift


def _ada_kernel(c_ref, w_ref, b_ref, o_ref):
    c = c_ref[...]
    act = (c * _sigmoid(c)).astype(BF16)
    o_ref[...] = _dot(act, w_ref[...].astype(BF16)) + b_ref[...]


def _rope_kernel(pos_ref, invf_ref, o_ref):
    ang = pos_ref[...] * invf_ref[...]
    c = jnp.cos(ang)
    s = jnp.sin(ang)
    lane = lax.broadcasted_iota(jnp.int32, ang.shape, 1)
    c64 = pltpu.roll(c, 64, 1)
    s64 = pltpu.roll(s, 64, 1)
    o_ref[:, 0:LANES] = jnp.where(lane < 64, c, c64)
    o_ref[:, LANES:2 * LANES] = jnp.where(lane < 64, -s, s64)

    def tile4(v, v64):
        return jnp.where(lane < 32, v64,
                         jnp.where(lane < 64, pltpu.roll(v, 96, 1),
                                   jnp.where(lane < 96, v, pltpu.roll(v, 32, 1))))

    cs = tile4(c, c64)
    ss = tile4(s, s64)
    first_half = (lane & 63) < 32
    o_ref[:, 2 * LANES:3 * LANES] = cs
    o_ref[:, 3 * LANES:4 * LANES] = jnp.where(first_half, -ss, 0.0)
    o_ref[:, 4 * LANES:5 * LANES] = jnp.where(first_half, 0.0, ss)


def _prep_kernel(c_ref, w_ref, b_ref, pos_ref, invf_ref, mod_ref, rope_ref):
    _ada_kernel(c_ref, w_ref, b_ref, mod_ref)
    _rope_kernel(pos_ref, invf_ref, rope_ref)


def _prep(c_pad, w_ada, b_ada, positions):
    rows = c_pad.shape[0]
    cols = N_MOD * D_MODEL
    blocks = cols // ADA_COLS
    steps = DEPTH * blocks
    n = positions.size
    assert n % steps == 0
    tm = n // steps
    inv_r = 1.0 / (ROPE_THETA ** (jnp.arange(0, RET_DK, 2, dtype=F32) / RET_DK))
    inv_s = 1.0 / (ROPE_THETA ** (jnp.arange(0, SWA_HEAD_DIM, 2, dtype=F32) / SWA_HEAD_DIM))
    invf = jnp.concatenate([inv_r, inv_s, jnp.zeros((LANES - 96,), F32)]).reshape(1, LANES)
    pos = positions.astype(F32).reshape(n, 1)
    return pl.pallas_call(
        _prep_kernel,
        grid=(steps,),
        in_specs=[
            pl.BlockSpec((rows, D_MODEL), lambda i: (0, 0)),
            pl.BlockSpec((None, D_MODEL, ADA_COLS), lambda i: (i // blocks, 0, i % blocks)),
            pl.BlockSpec((None, 1, ADA_COLS), lambda i: (i // blocks, 0, i % blocks)),
            pl.BlockSpec((tm, 1), lambda i: (i, 0)),
            pl.BlockSpec((1, LANES), lambda i: (0, 0)),
        ],
        out_specs=[pl.BlockSpec((None, rows, ADA_COLS), lambda i: (i // blocks, 0, i % blocks)),
                   pl.BlockSpec((tm, ROPE_COLS), lambda i: (i, 0))],
        out_shape=[jax.ShapeDtypeStruct((DEPTH, rows, cols), F32),
                   jax.ShapeDtypeStruct((n, ROPE_COLS), F32)],
        compiler_params=_params(),
        name="prep",
    )(c_pad, w_ada, b_ada.reshape(DEPTH, 1, cols), pos, invf)


def _ffn_kernel(layer, mod_row, final, x_ref, xn_ref, gain_ref, mod_ref, modn_ref, wup_hbm, wdn_hbm, *rest):
    if final:
        fgain_ref, o_ref, act_ref, h_ref, wup_ref, wdn_ref, sem = rest
    else:
        o_ref, act_ref, h_ref, wup_ref, wdn_ref, sem = rest
    n_up = D_FF // FF_CHUNK
    dn_rows = D_FF // FFN_DOWN_PARTS

    def up_copy(j, half):
        cols = pl.ds(half * D_FF + j * FF_CHUNK, FF_CHUNK)
        return pltpu.make_async_copy(wup_hbm.at[layer, :, cols], wup_ref.at[:, cols],
                                     sem.at[2 * j + half])

    def down_copy(p):
        rows = pl.ds(p * dn_rows, dn_rows)
        return pltpu.make_async_copy(wdn_hbm.at[layer, rows, :], wdn_ref.at[rows, :],
                                     sem.at[2 * n_up + p])

    def normed(xr, mr):
        return _norm_mod(xr[...], gain_ref[...], mr[mod_row:mod_row + 1, :],
                         mr[mod_row + 1:mod_row + 2, :]).astype(BF16)

    def up_chunk(j):
        lo = j * FF_CHUNK
        g = _wdot(h_ref[...], wup_ref[:, lo:lo + FF_CHUNK])
        u = _wdot(h_ref[...], wup_ref[:, D_FF + lo:D_FF + lo + FF_CHUNK])
        act_ref[:, lo:lo + FF_CHUNK] = (g * _sigmoid(g) * u).astype(BF16)

    def finish_tile():
        y = _wdot(act_ref[...], wdn_ref[...])
        h_ref[...] = normed(xn_ref, modn_ref)
        out = x_ref[...] + (0.5 * mod_ref[mod_row + 2:mod_row + 3, :]) * y
        if final:
            out = out * lax.rsqrt(jnp.mean(out * out, axis=-1, keepdims=True) + NORM_EPS)
            out = out * fgain_ref[...]
        o_ref[...] = out
        up_chunk(0)

    @pl.when(pl.program_id(0) == 0)
    def _():
        for j in range(n_up):
            up_copy(j, 0).start()
            up_copy(j, 1).start()
        for p in range(FFN_DOWN_PARTS):
            down_copy(p).start()
        h_ref[...] = normed(x_ref, mod_ref)
        for j in range(n_up):
            up_copy(j, 0).wait()
            up_copy(j, 1).wait()
            up_chunk(j)
        for p in range(FFN_DOWN_PARTS):
            down_copy(p).wait()
        finish_tile()

    @pl.when(pl.program_id(0) > 0)
    def _():
        for j in range(1, n_up):
            up_chunk(j)
        finish_tile()


def _tile_specs(n, tm, batch, layer):
    tiles = n // tm
    tiles_per_batch = tiles // batch
    nxt = lambda i: jnp.minimum(i + 1, tiles - 1)
    return [
        pl.BlockSpec((tm, D_MODEL), lambda i: (i, 0)),
        pl.BlockSpec((tm, D_MODEL), lambda i: (nxt(i), 0)),
        _resident((1, D_MODEL)),
        pl.BlockSpec((None, N_MOD, D_MODEL), lambda i: (layer * batch + i // tiles_per_batch, 0, 0)),
        pl.BlockSpec((None, N_MOD, D_MODEL), lambda i: (layer * batch + nxt(i) // tiles_per_batch, 0, 0)),
    ]


def _ffn(x, gain, mod, layer, mod_row, w_up, w_down, batch, final_gain=None):
    n = x.shape[0]
    tm = TOKEN_TILE
    final = final_gain is not None
    in_specs = _tile_specs(n, tm, batch, layer) + [
        pl.BlockSpec(memory_space=pl.ANY),
        pl.BlockSpec(memory_space=pl.ANY),
    ]
    args = [x, x, gain.reshape(1, D_MODEL), mod, mod, w_up, w_down]
    if final:
        in_specs.append(_resident((1, D_MODEL)))
        args.append(final_gain.reshape(1, D_MODEL))
    return pl.pallas_call(
        functools.partial(_ffn_kernel, layer, mod_row, final),
        grid=(n // tm,),
        in_specs=in_specs,
        out_specs=pl.BlockSpec((tm, D_MODEL), lambda i: (i, 0)),
        out_shape=jax.ShapeDtypeStruct((n, D_MODEL), F32),
        scratch_shapes=[pltpu.VMEM((tm, D_FF), BF16), pltpu.VMEM((tm, D_MODEL), BF16),
                        pltpu.VMEM((D_MODEL, 2 * D_FF), F32), pltpu.VMEM((D_FF, D_MODEL), F32),
                        pltpu.SemaphoreType.DMA((2 * (D_FF // FF_CHUNK) + FFN_DOWN_PARTS,))],
        compiler_params=_params(),
        name="ffn",
    )(*args)


def _ret_tables():
    h = np.arange(RET_HEADS, dtype=np.float64)
    log_gamma = np.log1p(-(2.0 ** (-5.0 - h)))
    idx = np.arange(RET_BLOCK, dtype=np.float64)
    rel = idx[:, None] - idx[None, :]
    decay = np.where(rel[None] >= 0, np.exp(np.maximum(rel, 0.0)[None] * log_gamma[:, None, None]), 0.0)
    decay = decay * (RET_DK ** -0.5)
    zeta = np.exp((RET_BLOCK - 1 - idx)[None, :] * log_gamma[:, None])
    xi = np.exp((idx + 1)[None, :] * log_gamma[:, None]) * (RET_DK ** -0.5)
    zeta = np.broadcast_to(zeta[:, :, None], (RET_HEADS, RET_BLOCK, RET_DK))
    xi = np.broadcast_to(xi[:, :, None], (RET_HEADS, RET_BLOCK, RET_DK))
    gamma_block = tuple(float(v) for v in np.exp(RET_BLOCK * log_gamma))
    return (jnp.asarray(decay, F32), jnp.asarray(zeta, F32), jnp.asarray(xi, F32), gamma_block)


def _retention_tile(first, gamma_block, q_ref, k_ref, v_ref, g_ref, dec_ref, zeta_ref, xi_ref,
                    o_ref, state_ref, raw_ref):
    rows_total = q_ref.shape[0]
    for c in range(rows_total // RET_BLOCK):
        rows = slice(c * RET_BLOCK, (c + 1) * RET_BLOCK)
        for h in range(RET_HEADS):
            kcols = slice(h * RET_DK, (h + 1) * RET_DK)
            vcols = slice(h * RET_DV, (h + 1) * RET_DV)
            q = q_ref[rows, kcols]
            k = k_ref[rows, kcols]
            v = v_ref[rows, vcols]
            state = state_ref[h]
            if c == 0:
                state = jnp.where(first, 0.0, state)
            scores = _dot_nt(q, k) * dec_ref[h]
            q_in = q.astype(F32) * xi_ref[h]
            kz = (k.astype(F32) * zeta_ref[h]).astype(BF16)
            lhs = jnp.concatenate(
                [jnp.concatenate([scores.astype(BF16), q_in.astype(BF16)], axis=1),
                 jnp.concatenate([kz.T, jnp.zeros((RET_DK, RET_DK), BF16)], axis=1)], axis=0)
            rhs = jnp.concatenate([v, state.astype(BF16)], axis=0)
            res = _dot(lhs, rhs)
            raw_ref[rows, vcols] = res[:RET_BLOCK]
            state_ref[h] = state * gamma_block[h] + res[RET_BLOCK:]

    for r in range(rows_total // NORM_ROWS):
        rows = slice(r * NORM_ROWS, (r + 1) * NORM_ROWS)
        for h in range(RET_HEADS):
            vcols = slice(h * RET_DV, (h + 1) * RET_DV)
            out = raw_ref[rows, vcols]
            mu = jnp.mean(out, axis=-1, keepdims=True)
            dev = out - mu
            var = jnp.mean(dev * dev, axis=-1, keepdims=True)
            gate = g_ref[rows, vcols].astype(F32)
            o_ref[rows, vcols] = (gate * _sigmoid(gate) * (dev * lax.rsqrt(var + NORM_EPS))).astype(BF16)


def _swa_tile(first, sink_ref, q_ref, kv_ref, kvp_ref, o_ref):
    low = lax.broadcasted_iota(jnp.int32, (CHUNK, LANES), 1) < SWA_HEAD_DIM
    qi = lax.broadcasted_iota(jnp.int32, (CHUNK, CHUNK), 0)
    kj = lax.broadcasted_iota(jnp.int32, (CHUNK, CHUNK), 1)
    tri = kj <= qi
    zero = jnp.zeros((CHUNK, LANES), F32)
    ones_lo = jnp.where(low, 1.0, 0.0)
    ones_hi = jnp.where(low, 0.0, 1.0)

    def pieces(kvb):
        kf = kvb[:, :LANES].astype(F32)
        vf = kvb[:, LANES:].astype(F32)
        ks = pltpu.roll(kf, SWA_HEAD_DIM, 1)
        vs = pltpu.roll(vf, SWA_HEAD_DIM, 1)
        out = []
        for lo_src_k, hi_src_k, lo_src_v, hi_src_v in ((kf, ks, vf, vs), (ks, kf, vs, vf)):
            k_lo = jnp.where(low, lo_src_k, zero).astype(BF16)
            k_hi = jnp.where(low, zero, hi_src_k).astype(BF16)
            vo_lo = jnp.concatenate([jnp.where(low, lo_src_v, zero), ones_lo], axis=1).astype(BF16)
            vo_hi = jnp.concatenate([jnp.where(low, zero, hi_src_v), ones_hi], axis=1).astype(BF16)
            out.append((k_lo, k_hi, vo_lo, vo_hi))
        return out

    prev = pieces(kvp_ref[...])
    for j in range(q_ref.shape[0] // CHUNK):
        rows = slice(j * CHUNK, (j + 1) * CHUNK)
        cur = pieces(kv_ref[rows, :])
        kb = [jnp.concatenate([prev[g][0], cur[g][0], prev[g][1], cur[g][1]], axis=0)
              for g in range(SWA_KV_HEADS)]
        vo = [jnp.concatenate([prev[g][2], cur[g][2], prev[g][3], cur[g][3]], axis=0)
              for g in range(SWA_KV_HEADS)]
        for p in range(SWA_Q_HEADS // 2):
            g = p // (SWA_Q_HEADS // 2 // SWA_KV_HEADS)
            s = _dot_nt(q_ref[rows, p * LANES:(p + 1) * LANES], kb[g])
            probs = []
            sink_terms = []
            for hh in range(2):
                sink = sink_ref[2 * p + hh]
                s_prev = s[:, (2 * hh) * CHUNK:(2 * hh + 1) * CHUNK]
                s_cur = s[:, (2 * hh + 1) * CHUNK:(2 * hh + 2) * CHUNK]
                if j == 0:
                    s_prev = jnp.where(first, -jnp.inf, s_prev)
                comb = jnp.where(tri, s_cur, s_prev)
                m = jnp.maximum(jnp.max(comb, axis=-1, keepdims=True), sink)
                pe = jnp.exp(comb - m)
                p_cur = jnp.where(tri, pe, 0.0)
                probs += [pe - p_cur, p_cur]
                sink_terms.append(jnp.exp(sink - m))
            res = _dot(jnp.concatenate(probs, axis=1).astype(BF16), vo[g])
            den = res[:, LANES:] + jnp.where(low, sink_terms[0], sink_terms[1])
            o_ref[rows, p * LANES:(p + 1) * LANES] = (res[:, :LANES] / den).astype(BF16)
        prev = cur


def _proj_kernel(layer, x_ref, xn_ref, gain_ref, mod_ref, modn_ref, rope_ref, w_hbm,
                 rq_ref, rk_ref, rv_ref, rg_ref, sq_ref, skv_ref, gr_ref, gs_ref,
                 h_ref, carry_ref, w_ref, sem):
    def normed(xr, mr):
        return _norm_mod(xr[...], gain_ref[...], mr[3:4, :], mr[4:5, :]).astype(BF16)

    cos_r = rope_ref[:, 0:LANES]
    sin_r = rope_ref[:, LANES:2 * LANES]
    cos_s = rope_ref[:, 2 * LANES:3 * LANES]
    sin_a = rope_ref[:, 3 * LANES:4 * LANES]
    sin_b = rope_ref[:, 4 * LANES:5 * LANES]

    def rope_ret(v):
        return v * cos_r + pltpu.roll(v, 64, 1) * sin_r

    def rope_swa(v):
        return v * cos_s + pltpu.roll(v, 96, 1) * sin_a + pltpu.roll(v, 32, 1) * sin_b

    def rope_swa_q(v):
        return rope_swa(v) * (SWA_HEAD_DIM ** -0.5)

    def project(off, width, out_ref, out_off=0, rope=None):
        y = _wdot(h_ref[...], w_ref[:, off:off + width])
        if rope is None:
            out_ref[:, out_off:out_off + width] = y.astype(out_ref.dtype)
        else:
            for s in range(width // LANES):
                slab = y[:, s * LANES:(s + 1) * LANES]
                out_ref[:, out_off + s * LANES:out_off + (s + 1) * LANES] = rope(slab).astype(out_ref.dtype)

    def seg_skv():
        project(OFF_SK, SWA_KV_WIDTH, skv_ref, rope=rope_swa)
        project(OFF_SV, SWA_KV_WIDTH, skv_ref, out_off=SWA_KV_WIDTH)

    segments = (
        (OFF_RV, RET_V_WIDTH, None),
        (OFF_RQ, RET_QK_WIDTH, lambda: project(OFF_RQ, RET_QK_WIDTH, rq_ref, rope=rope_ret)),
        (OFF_RK, RET_QK_WIDTH, lambda: project(OFF_RK, RET_QK_WIDTH, rk_ref, rope=rope_ret)),
        (OFF_SQ, SWA_Q_WIDTH, lambda: project(OFF_SQ, SWA_Q_WIDTH, sq_ref, rope=rope_swa_q)),
        (OFF_SK, 2 * SWA_KV_WIDTH, seg_skv),
        (OFF_RG, RET_V_WIDTH, lambda: project(OFF_RG, RET_V_WIDTH, rg_ref)),
        (OFF_GR, D_MODEL, lambda: project(OFF_GR, D_MODEL, gr_ref)),
        (OFF_GS, D_MODEL, lambda: project(OFF_GS, D_MODEL, gs_ref)),
    )
    assert len(segments) == PROJ_SEGMENTS

    def w_copy(s):
        cols = pl.ds(segments[s][0], segments[s][1])
        return pltpu.make_async_copy(w_hbm.at[layer, :, cols], w_ref.at[:, cols], sem.at[s])

    def tile_body(wait):
        rv_ref[...] = carry_ref[...]
        for s in range(1, len(segments)):
            if wait:
                w_copy(s).wait()
            segments[s][2]()
        h_ref[...] = normed(xn_ref, modn_ref)
        project(OFF_RV, RET_V_WIDTH, carry_ref)

    @pl.when(pl.program_id(0) == 0)
    def _():
        for s in range(len(segments)):
            w_copy(s).start()
        h_ref[...] = normed(x_ref, mod_ref)
        w_copy(0).wait()
        project(OFF_RV, RET_V_WIDTH, carry_ref)
        tile_body(wait=True)

    @pl.when(pl.program_id(0) > 0)
    def _():
        tile_body(wait=False)


def _proj(x, gain, mod, layer, rope, w_in, batch):
    n = x.shape[0]
    tm = TOKEN_TILE
    widths = (RET_QK_WIDTH, RET_QK_WIDTH, RET_V_WIDTH, RET_V_WIDTH, SWA_Q_WIDTH,
              2 * SWA_KV_WIDTH, D_MODEL, D_MODEL)
    return pl.pallas_call(
        functools.partial(_proj_kernel, layer),
        grid=(n // tm,),
        in_specs=_tile_specs(n, tm, batch, layer) + [
            pl.BlockSpec((tm, ROPE_COLS), lambda i: (i, 0)),
            pl.BlockSpec(memory_space=pl.ANY),
        ],
        out_specs=[pl.BlockSpec((tm, w), lambda i: (i, 0)) for w in widths],
        out_shape=[jax.ShapeDtypeStruct((n, w), BF16) for w in widths],
        scratch_shapes=[pltpu.VMEM((tm, D_MODEL), BF16), pltpu.VMEM((tm, RET_V_WIDTH), BF16),
                        pltpu.VMEM((D_MODEL, IN_COLS), F32), pltpu.SemaphoreType.DMA((PROJ_SEGMENTS,))],
        compiler_params=_params(),
        name="proj",
    )(x, x, gain.reshape(1, D_MODEL), mod, mod, rope, w_in)


def _ret_kernel(gamma_block, q_ref, k_ref, v_ref, g_ref, dec_ref, zeta_ref, xi_ref, o_ref, state_ref,
                raw_ref):
    _retention_tile(pl.program_id(1) == 0, gamma_block, q_ref, k_ref, v_ref, g_ref, dec_ref, zeta_ref,
                    xi_ref, o_ref, state_ref, raw_ref)


def _retention(rq, rk, rv, rg, batch):
    n = rq.shape[0]
    steps = n // batch // SEQ_TILE
    decay, zeta, xi, gamma_block = _ret_tables()
    row_block = lambda w: pl.BlockSpec((SEQ_TILE, w), lambda b, t: (b * steps + t, 0))
    return pl.pallas_call(
        functools.partial(_ret_kernel, gamma_block),
        grid=(batch, steps),
        in_specs=[row_block(RET_QK_WIDTH), row_block(RET_QK_WIDTH), row_block(RET_V_WIDTH),
                  row_block(RET_V_WIDTH),
                  _resident((RET_HEADS, RET_BLOCK, RET_BLOCK)), _resident((RET_HEADS, RET_BLOCK, RET_DK)),
                  _resident((RET_HEADS, RET_BLOCK, RET_DK))],
        out_specs=row_block(RET_V_WIDTH),
        out_shape=jax.ShapeDtypeStruct((n, RET_V_WIDTH), BF16),
        scratch_shapes=[pltpu.VMEM((RET_HEADS, RET_DK, RET_DV), F32),
                        pltpu.VMEM((SEQ_TILE, RET_V_WIDTH), F32)],
        compiler_params=_params2(),
        name="retention",
    )(rq, rk, rv, rg, decay, zeta, xi)


def _swa_kernel(sink_ref, q_ref, kv_ref, kvp_ref, o_ref):
    _swa_tile(pl.program_id(1) == 0, sink_ref, q_ref, kv_ref, kvp_ref, o_ref)


def _swa(sinks, sq, skv, batch):
    n = sq.shape[0]
    steps = n // batch // SEQ_TILE
    per = SEQ_TILE // CHUNK
    return pl.pallas_call(
        _swa_kernel,
        grid=(batch, steps),
        in_specs=[
            pl.BlockSpec(memory_space=pltpu.SMEM),
            pl.BlockSpec((SEQ_TILE, SWA_Q_WIDTH), lambda b, t: (b * steps + t, 0)),
            pl.BlockSpec((SEQ_TILE, 2 * SWA_KV_WIDTH), lambda b, t: (b * steps + t, 0)),
            pl.BlockSpec((CHUNK, 2 * SWA_KV_WIDTH),
                         lambda b, t: ((b * steps + t) * per - jnp.minimum(t, 1), 0)),
        ],
        out_specs=pl.BlockSpec((SEQ_TILE, SWA_Q_WIDTH), lambda b, t: (b * steps + t, 0)),
        out_shape=jax.ShapeDtypeStruct((n, SWA_Q_WIDTH), BF16),
        compiler_params=_params2(),
        name="swa",
    )(sinks, sq, skv, skv)


def _merge_kernel(x_ref, mod_ref, ret_ref, swa_ref, gr_ref, gs_ref, wr_ref, ws_ref, wo_ref, o_ref):
    br = _wdot(ret_ref[...], wr_ref[...])
    bs = _wdot(swa_ref[...], ws_ref[...])
    merged = _sigmoid(gr_ref[...].astype(F32)) * br + _sigmoid(gs_ref[...].astype(F32)) * bs
    y = _wdot(merged.astype(BF16), wo_ref[...])
    o_ref[...] = x_ref[...] + mod_ref[5:6, :] * y


def _merge(x, mod, layer, ret, swa, gr, gs, w_ret, w_swa, w_out, batch):
    n = x.shape[0]
    tm = MERGE_TILE
    tiles_per_batch = n // batch // tm
    rows = lambda w: pl.BlockSpec((tm, w), lambda i: (i, 0))
    return pl.pallas_call(
        _merge_kernel,
        grid=(n // tm,),
        in_specs=[
            rows(D_MODEL),
            pl.BlockSpec((None, N_MOD, D_MODEL), lambda i: (layer * batch + i // tiles_per_batch, 0, 0)),
            rows(RET_V_WIDTH), rows(SWA_Q_WIDTH), rows(D_MODEL), rows(D_MODEL),
            _layer_resident(layer, (RET_V_WIDTH, D_MODEL)), _layer_resident(layer, (SWA_Q_WIDTH, D_MODEL)),
            _layer_resident(layer, (D_MODEL, D_MODEL)),
        ],
        out_specs=rows(D_MODEL),
        out_shape=jax.ShapeDtypeStruct((n, D_MODEL), F32),
        compiler_params=_params(),
        name="merge",
    )(x, mod, ret, swa, gr, gs, w_ret, w_swa, w_out)


def kernel(x, c, positions, norm_ffn1, norm_mix, norm_ffn2, final_norm, w_ada, b_ada,
           ffn1_w_up, ffn1_w_down, ffn2_w_up, ffn2_w_down, w_in, sinks,
           w_branch_ret, w_branch_swa, w_out):
    batch, seq, d = x.shape
    n = batch * seq
    assert d == D_MODEL and seq % max(TOKEN_TILE, MERGE_TILE, SEQ_TILE) == 0
    xf = x.reshape(n, d)

    pad_rows = -batch % 8
    c_pad = jnp.pad(c, ((0, pad_rows), (0, 0)))
    mod, rope = _prep(c_pad, w_ada, b_ada, positions)
    mod = mod[:, :batch].reshape(DEPTH * batch, N_MOD, D_MODEL)

    for l in range(DEPTH):
        xf = _ffn(xf, norm_ffn1[l], mod, l, 0, ffn1_w_up, ffn1_w_down, batch)
        rq, rk, rv, rg, sq, skv, gr, gs = _proj(xf, norm_mix[l], mod, l, rope, w_in, batch)
        ret = _retention(rq, rk, rv, rg, batch)
        swa = _swa(sinks[l], sq, skv, batch)
        xf = _merge(xf, mod, l, ret, swa, gr, gs, w_branch_ret, w_branch_swa, w_out, batch)
        final_gain = final_norm if l == DEPTH - 1 else None
        xf = _ffn(xf, norm_ffn2[l], mod, l, 6, ffn2_w_up, ffn2_w_down, batch, final_gain)
    return xf.reshape(batch, seq, d)
```

```python
import functools

import numpy as np
import jax
import jax.numpy as jnp
from jax import lax
from jax.experimental import pallas as pl
from jax.experimental.pallas import tpu as pltpu

F32 = jnp.float32
BF16 = jnp.bfloat16

D_MODEL = 1024
DEPTH = 2
RET_HEADS = 4
RET_DK = 128
RET_DV = 256
CHUNK = 128
SWA_Q_HEADS = 16
SWA_KV_HEADS = 2
SWA_HEAD_DIM = 64
D_FF = 2816
ROPE_THETA = 10000.0
NORM_EPS = 1e-6
N_MOD = 9
LOG2_E = 1.4426950408889634

RET_QK_WIDTH = RET_HEADS * RET_DK
RET_V_WIDTH = RET_HEADS * RET_DV
SWA_Q_WIDTH = SWA_Q_HEADS * SWA_HEAD_DIM
SWA_KV_WIDTH = SWA_KV_HEADS * SWA_HEAD_DIM
OFF_RQ = 0
OFF_RK = OFF_RQ + RET_QK_WIDTH
OFF_RV = OFF_RK + RET_QK_WIDTH
OFF_RG = OFF_RV + RET_V_WIDTH
OFF_SQ = OFF_RG + RET_V_WIDTH
OFF_SK = OFF_SQ + SWA_Q_WIDTH
OFF_SV = OFF_SK + SWA_KV_WIDTH
OFF_GR = OFF_SV + SWA_KV_WIDTH
OFF_GS = OFF_GR + D_MODEL
IN_COLS = OFF_GS + D_MODEL

LANES = 128
ROPE_COLS = 5 * LANES

TOKEN_TILE = 512
MERGE_TILE = 1024
RET_TILE = 1024
SWA_TILE = 2048
FF_CHUNK = 256
FFN_DOWN_PARTS = 4
PROJ_SEGMENTS = 8
NORM_ROWS = 32
RET_BLOCK = 256
ADA_COLS = 1152
VMEM_LIMIT = 56 * 1024 * 1024


def _sigmoid(v):
    return 1.0 / (1.0 + jnp.exp(-v))


def _dot(a, b):
    return jnp.dot(a, b, preferred_element_type=F32)


def _dot_nt(a, b):
    return lax.dot_general(a, b, (((1,), (1,)), ((), ())), preferred_element_type=F32)


def _resident(shape):
    zeros = (0,) * len(shape)
    return pl.BlockSpec(shape, lambda *_: zeros, pipeline_mode=pl.Buffered(1))


def _layer_resident(layer, shape):
    zeros = (0,) * len(shape)
    return pl.BlockSpec((None,) + tuple(shape), lambda *_: (layer,) + zeros,
                        pipeline_mode=pl.Buffered(1))


def _wdot(a, w):
    return jnp.dot(a, w.astype(BF16), preferred_element_type=F32)


def _params():
    return pltpu.CompilerParams(dimension_semantics=("arbitrary",), vmem_limit_bytes=VMEM_LIMIT)


def _params2():
    return pltpu.CompilerParams(dimension_semantics=("arbitrary", "arbitrary"),
                                vmem_limit_bytes=VMEM_LIMIT)


def _norm_mod(x, gain, shift, scale):
    y = x * lax.rsqrt(jnp.mean(x * x, axis=-1, keepdims=True) + NORM_EPS)
    return (y * gain) * (1.0 + scale) + shift


def _ada_kernel(c_ref, w_ref, b_ref, o_ref):
    c = c_ref[...]
    act = (c * _sigmoid(c)).astype(BF16)
    o_ref[...] = _dot(act, w_ref[...].astype(BF16)) + b_ref[...]


def _rope_kernel(pos_ref, invf_ref, o_ref):
    half = pos_ref.shape[0] // 2
    lane = lax.broadcasted_iota(jnp.int32, (half, LANES), 1)
    low = lane < 64
    pos = jnp.where(low, pos_ref[0:half, :], pos_ref[half:, :])
    ang = pos * invf_ref[...]
    c = jnp.cos(ang)
    s = jnp.sin(ang)
    c64 = pltpu.roll(c, 64, 1)
    s64 = pltpu.roll(s, 64, 1)
    first_half = (lane & 63) < 32
    even = 2 * (lane & 31)
    for t in range(2):
        rows = slice(t * half, (t + 1) * half)
        mine = low if t == 0 else jnp.logical_not(low)
        o_ref[rows, 0:LANES] = jnp.where(mine, c, c64)
        o_ref[rows, LANES:2 * LANES] = jnp.where(low, -1.0, 1.0) * jnp.where(mine, s, s64)
        cs = jnp.take_along_axis(c, even + 64 * t, axis=1)
        ss = jnp.take_along_axis(s, even + 64 * t, axis=1)
        o_ref[rows, 2 * LANES:3 * LANES] = cs
        o_ref[rows, 3 * LANES:4 * LANES] = jnp.where(first_half, -ss, 0.0)
        o_ref[rows, 4 * LANES:5 * LANES] = jnp.where(first_half, 0.0, ss)


def _prep_kernel(c_ref, w_ref, b_ref, pos_ref, invf_ref, mod_ref, rope_ref):
    _ada_kernel(c_ref, w_ref, b_ref, mod_ref)
    _rope_kernel(pos_ref, invf_ref, rope_ref)


def _prep(c_pad, w_ada, b_ada, positions):
    rows = c_pad.shape[0]
    cols = N_MOD * D_MODEL
    blocks = cols // ADA_COLS
    steps = DEPTH * blocks
    n = positions.size
    assert n % steps == 0
    tm = n // steps
    inv_r = 1.0 / (ROPE_THETA ** (jnp.arange(0, RET_DK, 2, dtype=F32) / RET_DK))
    invf = jnp.concatenate([inv_r, inv_r]).reshape(1, LANES)
    pos = positions.astype(F32).reshape(n, 1)
    return pl.pallas_call(
        _prep_kernel,
        grid=(steps,),
        in_specs=[
            pl.BlockSpec((rows, D_MODEL), lambda i: (0, 0)),
            pl.BlockSpec((None, D_MODEL, ADA_COLS), lambda i: (i // blocks, 0, i % blocks)),
            pl.BlockSpec((None, 1, ADA_COLS), lambda i: (i // blocks, 0, i % blocks)),
            pl.BlockSpec((tm, 1), lambda i: (i, 0)),
            pl.BlockSpec((1, LANES), lambda i: (0, 0)),
        ],
        out_specs=[pl.BlockSpec((None, rows, ADA_COLS), lambda i: (i // blocks, 0, i % blocks)),
                   pl.BlockSpec((tm, ROPE_COLS), lambda i: (i, 0))],
        out_shape=[jax.ShapeDtypeStruct((DEPTH, rows, cols), F32),
                   jax.ShapeDtypeStruct((n, ROPE_COLS), F32)],
        compiler_params=_params(),
        name="prep",
    )(c_pad, w_ada, b_ada.reshape(DEPTH, 1, cols), pos, invf)


def _ffn_kernel(layer, mod_row, final, x_ref, xn_ref, gain_ref, mod_ref, modn_ref, wup_hbm, wdn_hbm, *rest):
    if final:
        fgain_ref, o_ref, act_ref, h_ref, wup_ref, wdn_ref, sem = rest
    else:
        o_ref, act_ref, h_ref, wup_ref, wdn_ref, sem = rest
    n_up = D_FF // FF_CHUNK
    dn_rows = D_FF // FFN_DOWN_PARTS

    def up_copy(j, half):
        cols = pl.ds(half * D_FF + j * FF_CHUNK, FF_CHUNK)
        return pltpu.make_async_copy(wup_hbm.at[layer, :, cols], wup_ref.at[:, cols],
                                     sem.at[2 * j + half])

    def down_copy(p):
        rows = pl.ds(p * dn_rows, dn_rows)
        return pltpu.make_async_copy(wdn_hbm.at[layer, rows, :], wdn_ref.at[rows, :],
                                     sem.at[2 * n_up + p])

    def normed(xr, mr):
        return _norm_mod(xr[...], gain_ref[...], mr[mod_row:mod_row + 1, :],
                         mr[mod_row + 1:mod_row + 2, :]).astype(BF16)

    def up_chunk(j):
        lo = j * FF_CHUNK
        g = _wdot(h_ref[...], wup_ref[:, lo:lo + FF_CHUNK])
        u = _wdot(h_ref[...], wup_ref[:, D_FF + lo:D_FF + lo + FF_CHUNK])
        act_ref[:, lo:lo + FF_CHUNK] = (g * _sigmoid(g) * u).astype(BF16)

    def finish_tile():
        y = _wdot(act_ref[...], wdn_ref[...])
        h_ref[...] = normed(xn_ref, modn_ref)
        out = x_ref[...] + (0.5 * mod_ref[mod_row + 2:mod_row + 3, :]) * y
        if final:
            out = out * lax.rsqrt(jnp.mean(out * out, axis=-1, keepdims=True) + NORM_EPS)
            out = out * fgain_ref[...]
        o_ref[...] = out
        up_chunk(0)

    @pl.when(pl.program_id(0) == 0)
    def _():
        for j in range(n_up):
            up_copy(j, 0).start()
            up_copy(j, 1).start()
        for p in range(FFN_DOWN_PARTS):
            down_copy(p).start()
        h_ref[...] = normed(x_ref, mod_ref)
        for j in range(n_up):
            up_copy(j, 0).wait()
            up_copy(j, 1).wait()
            up_chunk(j)
        for p in range(FFN_DOWN_PARTS):
            down_copy(p).wait()
        finish_tile()

    @pl.when(pl.program_id(0) > 0)
    def _():
        for j in range(1, n_up):
            up_chunk(j)
        finish_tile()


def _tile_specs(n, tm, batch, layer):
    tiles = n // tm
    tiles_per_batch = tiles // batch
    nxt = lambda i: jnp.minimum(i + 1, tiles - 1)
    return [
        pl.BlockSpec((tm, D_MODEL), lambda i: (i, 0)),
        pl.BlockSpec((tm, D_MODEL), lambda i: (nxt(i), 0)),
        _resident((1, D_MODEL)),
        pl.BlockSpec((None, N_MOD, D_MODEL), lambda i: (layer * batch + i // tiles_per_batch, 0, 0)),
        pl.BlockSpec((None, N_MOD, D_MODEL), lambda i: (layer * batch + nxt(i) // tiles_per_batch, 0, 0)),
    ]


def _ffn(x, gain, mod, layer, mod_row, w_up, w_down, batch, final_gain=None):
    n = x.shape[0]
    tm = TOKEN_TILE
    final = final_gain is not None
    in_specs = _tile_specs(n, tm, batch, layer) + [
        pl.BlockSpec(memory_space=pl.ANY),
        pl.BlockSpec(memory_space=pl.ANY),
    ]
    args = [x, x, gain.reshape(1, D_MODEL), mod, mod, w_up, w_down]
    if final:
        in_specs.append(_resident((1, D_MODEL)))
        args.append(final_gain.reshape(1, D_MODEL))
    return pl.pallas_call(
        functools.partial(_ffn_kernel, layer, mod_row, final),
        grid=(n // tm,),
        in_specs=in_specs,
        out_specs=pl.BlockSpec((tm, D_MODEL), lambda i: (i, 0)),
        out_shape=jax.ShapeDtypeStruct((n, D_MODEL), F32),
        scratch_shapes=[pltpu.VMEM((tm, D_FF), BF16), pltpu.VMEM((tm, D_MODEL), BF16),
                        pltpu.VMEM((D_MODEL, 2 * D_FF), F32), pltpu.VMEM((D_FF, D_MODEL), F32),
                        pltpu.SemaphoreType.DMA((2 * (D_FF // FF_CHUNK) + FFN_DOWN_PARTS,))],
        compiler_params=_params(),
        name="ffn",
    )(*args)


def _ret_tables():
    h = np.arange(RET_HEADS, dtype=np.float64)
    log_gamma = np.log1p(-(2.0 ** (-5.0 - h)))
    idx = np.arange(RET_BLOCK, dtype=np.float64)
    rel = idx[:, None] - idx[None, :]
    decay = np.where(rel[None] >= 0, np.exp(np.maximum(rel, 0.0)[None] * log_gamma[:, None, None]), 0.0)
    decay = decay * (RET_DK ** -0.5)
    zeta = np.exp((RET_BLOCK - 1 - idx)[None, :] * log_gamma[:, None])
    xi = np.exp((idx + 1)[None, :] * log_gamma[:, None]) * (RET_DK ** -0.5)
    zeta = np.broadcast_to(zeta[:, :, None], (RET_HEADS, RET_BLOCK, RET_DK))
    xi = np.broadcast_to(xi[:, :, None], (RET_HEADS, RET_BLOCK, RET_DK))
    gamma_block = tuple(float(v) for v in np.exp(RET_BLOCK * log_gamma))
    return (jnp.asarray(decay, F32), jnp.asarray(zeta, F32), jnp.asarray(xi, F32), gamma_block)


def _retention_tile(first, gamma_block, q_ref, k_ref, v_ref, g_ref, dec_ref, zeta_ref, xi_ref,
                    o_ref, state_ref, raw_ref):
    rows_total = q_ref.shape[0]
    for c in range(rows_total // RET_BLOCK):
        rows = slice(c * RET_BLOCK, (c + 1) * RET_BLOCK)
        for h in range(RET_HEADS):
            kcols = slice(h * RET_DK, (h + 1) * RET_DK)
            vcols = slice(h * RET_DV, (h + 1) * RET_DV)
            q = q_ref[rows, kcols]
            k = k_ref[rows, kcols]
            v = v_ref[rows, vcols]
            state = state_ref[h]
            if c == 0:
                state = jnp.where(first, 0.0, state)
            scores = _dot_nt(q, k) * dec_ref[h]
            q_in = q.astype(F32) * xi_ref[h]
            kz = (k.astype(F32) * zeta_ref[h]).astype(BF16)
            lhs = jnp.concatenate(
                [jnp.concatenate([scores.astype(BF16), q_in.astype(BF16)], axis=1),
                 jnp.concatenate([kz.T, jnp.zeros((RET_DK, RET_DK), BF16)], axis=1)], axis=0)
            rhs = jnp.concatenate([v, state.astype(BF16)], axis=0)
            res = _dot(lhs, rhs)
            raw_ref[rows, vcols] = res[:RET_BLOCK]
            state_ref[h] = state * gamma_block[h] + res[RET_BLOCK:]

    for r in range(rows_total // NORM_ROWS):
        rows = slice(r * NORM_ROWS, (r + 1) * NORM_ROWS)
        for h in range(RET_HEADS):
            vcols = slice(h * RET_DV, (h + 1) * RET_DV)
            out = raw_ref[rows, vcols]
            mu = jnp.mean(out, axis=-1, keepdims=True)
            dev = out - mu
            var = jnp.mean(dev * dev, axis=-1, keepdims=True)
            gate = g_ref[rows, vcols].astype(F32)
            o_ref[rows, vcols] = (gate * _sigmoid(gate) * (dev * lax.rsqrt(var + NORM_EPS))).astype(BF16)


def _swa_tile(first, sink_ref, q_ref, kv_ref, kvp_ref, o_ref):
    low = lax.broadcasted_iota(jnp.int32, (CHUNK, LANES), 1) < SWA_HEAD_DIM
    qi = lax.broadcasted_iota(jnp.int32, (CHUNK, CHUNK), 0)
    kj = lax.broadcasted_iota(jnp.int32, (CHUNK, CHUNK), 1)
    tri = kj <= qi
    zero = jnp.zeros((CHUNK, LANES), F32)
    ones_lo = jnp.where(low, 1.0, 0.0)
    ones_hi = jnp.where(low, 0.0, 1.0)

    def pieces(kvb):
        kf = kvb[:, :LANES].astype(F32)
        vf = kvb[:, LANES:].astype(F32)
        ks = pltpu.roll(kf, SWA_HEAD_DIM, 1)
        vs = pltpu.roll(vf, SWA_HEAD_DIM, 1)
        out = []
        for lo_src_k, hi_src_k, lo_src_v, hi_src_v in ((kf, ks, vf, vs), (ks, kf, vs, vf)):
            k_lo = jnp.where(low, lo_src_k, zero).astype(BF16)
            k_hi = jnp.where(low, zero, hi_src_k).astype(BF16)
            vo_lo = jnp.concatenate([jnp.where(low, lo_src_v, zero), ones_lo], axis=1).astype(BF16)
            vo_hi = jnp.concatenate([jnp.where(low, zero, hi_src_v), ones_hi], axis=1).astype(BF16)
            out.append((k_lo, k_hi, vo_lo, vo_hi))
        return out

    prev = pieces(kvp_ref[...])
    for j in range(q_ref.shape[0] // CHUNK):
        rows = slice(j * CHUNK, (j + 1) * CHUNK)
        cur = pieces(kv_ref[rows, :])
        kb = [jnp.concatenate([prev[g][0], cur[g][0], prev[g][1], cur[g][1]], axis=0)
              for g in range(SWA_KV_HEADS)]
        vo = [jnp.concatenate([prev[g][2], cur[g][2], prev[g][3], cur[g][3]], axis=0)
              for g in range(SWA_KV_HEADS)]
        pairs_per_group = SWA_Q_HEADS // 2 // SWA_KV_HEADS
        for g in range(SWA_KV_HEADS):
            pairs = range(g * pairs_per_group, (g + 1) * pairs_per_group)
            q_all = jnp.concatenate([q_ref[rows, p * LANES:(p + 1) * LANES] for p in pairs], axis=0)
            s_all = _dot_nt(q_all, kb[g])
            probs_all = []
            sink_all = []
            for i, p in enumerate(pairs):
                s = s_all[i * CHUNK:(i + 1) * CHUNK]
                probs = []
                sink_terms = []
                for hh in range(2):
                    sink = sink_ref[2 * p + hh] * LOG2_E
                    s_prev = s[:, (2 * hh) * CHUNK:(2 * hh + 1) * CHUNK]
                    s_cur = s[:, (2 * hh + 1) * CHUNK:(2 * hh + 2) * CHUNK]
                    if j == 0:
                        s_prev = jnp.where(first, -jnp.inf, s_prev)
                    comb = jnp.where(tri, s_cur, s_prev)
                    m = jnp.maximum(jnp.max(comb, axis=-1, keepdims=True), sink)
                    pe = jnp.exp2(comb - m)
                    probs += [jnp.where(tri, 0.0, pe), jnp.where(tri, pe, 0.0)]
                    sink_terms.append(jnp.exp2(sink - m))
                probs_all.append(jnp.concatenate(probs, axis=1).astype(BF16))
                sink_all.append(jnp.where(low, sink_terms[0], sink_terms[1]))
            res_all = _dot(jnp.concatenate(probs_all, axis=0), vo[g])
            for i, p in enumerate(pairs):
                res = res_all[i * CHUNK:(i + 1) * CHUNK]
                den = res[:, LANES:] + sink_all[i]
                o_ref[rows, p * LANES:(p + 1) * LANES] = (res[:, :LANES] / den).astype(BF16)
        prev = cur


def _proj_kernel(layer, x_ref, xn_ref, gain_ref, mod_ref, modn_ref, rope_ref, w_hbm,
                 rq_ref, rk_ref, rv_ref, rg_ref, sq_ref, skv_ref, gr_ref, gs_ref,
                 h_ref, carry_ref, w_ref, sem):
    def normed(xr, mr):
        return _norm_mod(xr[...], gain_ref[...], mr[3:4, :], mr[4:5, :]).astype(BF16)

    cos_r = rope_ref[:, 0:LANES]
    sin_r = rope_ref[:, LANES:2 * LANES]
    cos_s = rope_ref[:, 2 * LANES:3 * LANES]
    sin_a = rope_ref[:, 3 * LANES:4 * LANES]
    sin_b = rope_ref[:, 4 * LANES:5 * LANES]

    def rope_ret(v):
        return v * cos_r + pltpu.roll(v, 64, 1) * sin_r

    def rope_swa(v):
        return v * cos_s + pltpu.roll(v, 96, 1) * sin_a + pltpu.roll(v, 32, 1) * sin_b

    def rope_swa_q(v):
        return rope_swa(v) * (SWA_HEAD_DIM ** -0.5 * LOG2_E)

    def project(off, width, out_ref, out_off=0, rope=None):
        y = _wdot(h_ref[...], w_ref[:, off:off + width])
        if rope is None:
            out_ref[:, out_off:out_off + width] = y.astype(out_ref.dtype)
        else:
            for s in range(width // LANES):
                slab = y[:, s * LANES:(s + 1) * LANES]
                out_ref[:, out_off + s * LANES:out_off + (s + 1) * LANES] = rope(slab).astype(out_ref.dtype)

    def seg_skv():
        project(OFF_SK, SWA_KV_WIDTH, skv_ref, rope=rope_swa)
        project(OFF_SV, SWA_KV_WIDTH, skv_ref, out_off=SWA_KV_WIDTH)

    segments = (
        (OFF_RV, RET_V_WIDTH, None),
        (OFF_RQ, RET_QK_WIDTH, lambda: project(OFF_RQ, RET_QK_WIDTH, rq_ref, rope=rope_ret)),
        (OFF_RK, RET_QK_WIDTH, lambda: project(OFF_RK, RET_QK_WIDTH, rk_ref, rope=rope_ret)),
        (OFF_SQ, SWA_Q_WIDTH, lambda: project(OFF_SQ, SWA_Q_WIDTH, sq_ref, rope=rope_swa_q)),
        (OFF_SK, 2 * SWA_KV_WIDTH, seg_skv),
        (OFF_RG, RET_V_WIDTH, lambda: project(OFF_RG, RET_V_WIDTH, rg_ref)),
        (OFF_GR, D_MODEL, lambda: project(OFF_GR, D_MODEL, gr_ref)),
        (OFF_GS, D_MODEL, lambda: project(OFF_GS, D_MODEL, gs_ref)),
    )
    assert len(segments) == PROJ_SEGMENTS

    def w_copy(s):
        cols = pl.ds(segments[s][0], segments[s][1])
        return pltpu.make_async_copy(w_hbm.at[layer, :, cols], w_ref.at[:, cols], sem.at[s])

    def tile_body(wait):
        rv_ref[...] = carry_ref[...]
        for s in range(1, len(segments)):
            if wait:
                w_copy(s).wait()
            segments[s][2]()
        h_ref[...] = normed(xn_ref, modn_ref)
        project(OFF_RV, RET_V_WIDTH, carry_ref)

    @pl.when(pl.program_id(0) == 0)
    def _():
        for s in range(len(segments)):
            w_copy(s).start()
        h_ref[...] = normed(x_ref, mod_ref)
        w_copy(0).wait()
        project(OFF_RV, RET_V_WIDTH, carry_ref)
        tile_body(wait=True)

    @pl.when(pl.program_id(0) > 0)
    def _():
        tile_body(wait=False)


def _proj(x, gain, mod, layer, rope, w_in, batch):
    n = x.shape[0]
    tm = TOKEN_TILE
    widths = (RET_QK_WIDTH, RET_QK_WIDTH, RET_V_WIDTH, RET_V_WIDTH, SWA_Q_WIDTH,
              2 * SWA_KV_WIDTH, D_MODEL, D_MODEL)
    return pl.pallas_call(
        functools.partial(_proj_kernel, layer),
        grid=(n // tm,),
        in_specs=_tile_specs(n, tm, batch, layer) + [
            pl.BlockSpec((tm, ROPE_COLS), lambda i: (i, 0)),
            pl.BlockSpec(memory_space=pl.ANY),
        ],
        out_specs=[pl.BlockSpec((tm, w), lambda i: (i, 0)) for w in widths],
        out_shape=[jax.ShapeDtypeStruct((n, w), BF16) for w in widths],
        scratch_shapes=[pltpu.VMEM((tm, D_MODEL), BF16), pltpu.VMEM((tm, RET_V_WIDTH), BF16),
                        pltpu.VMEM((D_MODEL, IN_COLS), F32), pltpu.SemaphoreType.DMA((PROJ_SEGMENTS,))],
        compiler_params=_params(),
        name="proj",
    )(x, x, gain.reshape(1, D_MODEL), mod, mod, rope, w_in)


def _ret_kernel(gamma_block, q_ref, k_ref, v_ref, g_ref, dec_ref, zeta_ref, xi_ref, o_ref, state_ref,
                raw_ref):
    _retention_tile(pl.program_id(1) == 0, gamma_block, q_ref, k_ref, v_ref, g_ref, dec_ref, zeta_ref,
                    xi_ref, o_ref, state_ref, raw_ref)


def _retention(rq, rk, rv, rg, batch):
    n = rq.shape[0]
    steps = n // batch // RET_TILE
    decay, zeta, xi, gamma_block = _ret_tables()
    row_block = lambda w: pl.BlockSpec((RET_TILE, w), lambda b, t: (b * steps + t, 0))
    return pl.pallas_call(
        functools.partial(_ret_kernel, gamma_block),
        grid=(batch, steps),
        in_specs=[row_block(RET_QK_WIDTH), row_block(RET_QK_WIDTH), row_block(RET_V_WIDTH),
                  row_block(RET_V_WIDTH),
                  _resident((RET_HEADS, RET_BLOCK, RET_BLOCK)), _resident((RET_HEADS, RET_BLOCK, RET_DK)),
                  _resident((RET_HEADS, RET_BLOCK, RET_DK))],
        out_specs=row_block(RET_V_WIDTH),
        out_shape=jax.ShapeDtypeStruct((n, RET_V_WIDTH), BF16),
        scratch_shapes=[pltpu.VMEM((RET_HEADS, RET_DK, RET_DV), F32),
                        pltpu.VMEM((RET_TILE, RET_V_WIDTH), F32)],
        compiler_params=_params2(),
        name="retention",
    )(rq, rk, rv, rg, decay, zeta, xi)


def _swa_kernel(sink_ref, q_ref, kv_ref, kvp_ref, o_ref):
    _swa_tile(pl.program_id(1) == 0, sink_ref, q_ref, kv_ref, kvp_ref, o_ref)


def _swa(sinks, sq, skv, batch):
    n = sq.shape[0]
    steps = n // batch // SWA_TILE
    per = SWA_TILE // CHUNK
    return pl.pallas_call(
        _swa_kernel,
        grid=(batch, steps),
        in_specs=[
            pl.BlockSpec(memory_space=pltpu.SMEM),
            pl.BlockSpec((SWA_TILE, SWA_Q_WIDTH), lambda b, t: (b * steps + t, 0)),
            pl.BlockSpec((SWA_TILE, 2 * SWA_KV_WIDTH), lambda b, t: (b * steps + t, 0)),
            pl.BlockSpec((CHUNK, 2 * SWA_KV_WIDTH),
                         lambda b, t: ((b * steps + t) * per - jnp.minimum(t, 1), 0)),
        ],
        out_specs=pl.BlockSpec((SWA_TILE, SWA_Q_WIDTH), lambda b, t: (b * steps + t, 0)),
        out_shape=jax.ShapeDtypeStruct((n, SWA_Q_WIDTH), BF16),
        compiler_params=_params2(),
        name="swa",
    )(sinks, sq, skv, skv)


def _merge_kernel(x_ref, mod_ref, ret_ref, swa_ref, gr_ref, gs_ref, wr_ref, ws_ref, wo_ref, o_ref):
    br = _wdot(ret_ref[...], wr_ref[...])
    bs = _wdot(swa_ref[...], ws_ref[...])
    merged = _sigmoid(gr_ref[...].astype(F32)) * br + _sigmoid(gs_ref[...].astype(F32)) * bs
    y = _wdot(merged.astype(BF16), wo_ref[...])
    o_ref[...] = x_ref[...] + mod_ref[5:6, :] * y


def _merge(x, mod, layer, ret, swa, gr, gs, w_ret, w_swa, w_out, batch):
    n = x.shape[0]
    tm = MERGE_TILE
    tiles_per_batch = n // batch // tm
    rows = lambda w: pl.BlockSpec((tm, w), lambda i: (i, 0))
    return pl.pallas_call(
        _merge_kernel,
        grid=(n // tm,),
        in_specs=[
            rows(D_MODEL),
            pl.BlockSpec((None, N_MOD, D_MODEL), lambda i: (layer * batch + i // tiles_per_batch, 0, 0)),
            rows(RET_V_WIDTH), rows(SWA_Q_WIDTH), rows(D_MODEL), rows(D_MODEL),
            _layer_resident(layer, (RET_V_WIDTH, D_MODEL)), _layer_resident(layer, (SWA_Q_WIDTH, D_MODEL)),
            _layer_resident(layer, (D_MODEL, D_MODEL)),
        ],
        out_specs=rows(D_MODEL),
        out_shape=jax.ShapeDtypeStruct((n, D_MODEL), F32),
        compiler_params=_params(),
        name="merge",
    )(x, mod, ret, swa, gr, gs, w_ret, w_swa, w_out)


def kernel(x, c, positions, norm_ffn1, norm_mix, norm_ffn2, final_norm, w_ada, b_ada,
           ffn1_w_up, ffn1_w_down, ffn2_w_up, ffn2_w_down, w_in, sinks,
           w_branch_ret, w_branch_swa, w_out):
    batch, seq, d = x.shape
    n = batch * seq
    assert d == D_MODEL and seq % max(TOKEN_TILE, MERGE_TILE, RET_TILE, SWA_TILE) == 0
    xf = x.reshape(n, d)

    pad_rows = -batch % 8
    c_pad = jnp.pad(c, ((0, pad_rows), (0, 0)))
    mod, rope = _prep(c_pad, w_ada, b_ada, positions)
    mod = mod[:, :batch].reshape(DEPTH * batch, N_MOD, D_MODEL)

    for l in range(DEPTH):
        xf = _ffn(xf, norm_ffn1[l], mod, l, 0, ffn1_w_up, ffn1_w_down, batch)
        rq, rk, rv, rg, sq, skv, gr, gs = _proj(xf, norm_mix[l], mod, l, rope, w_in, batch)
        ret = _retention(rq, rk, rv, rg, batch)
        swa = _swa(sinks[l], sq, skv, batch)
        xf = _merge(xf, mod, l, ret, swa, gr, gs, w_branch_ret, w_branch_swa, w_out, batch)
        final_gain = final_norm if l == DEPTH - 1 else None
        xf = _ffn(xf, norm_ffn2[l], mod, l, 6, ffn2_w_up, ffn2_w_down, batch, final_gain)
    return xf.reshape(batch, seq, d)
```

```python
import functools

import numpy as np
import jax
import jax.numpy as jnp
from jax import lax
from jax.experimental import pallas as pl
from jax.experimental.pallas import tpu as pltpu

F32 = jnp.float32
BF16 = jnp.bfloat16

D_MODEL = 1024
DEPTH = 2
RET_HEADS = 4
RET_DK = 128
RET_DV = 256
CHUNK = 128
SWA_Q_HEADS = 16
SWA_KV_HEADS = 2
SWA_HEAD_DIM = 64
D_FF = 2816
ROPE_THETA = 10000.0
NORM_EPS = 1e-6
N_MOD = 9
LOG2_E = 1.4426950408889634

RET_FREQS = RET_DK // 2
SWA_FREQS = SWA_HEAD_DIM // 2
RET_QK_WIDTH = RET_HEADS * RET_DK
RET_V_WIDTH = RET_HEADS * RET_DV
SWA_Q_WIDTH = SWA_Q_HEADS * SWA_HEAD_DIM
SWA_KV_WIDTH = SWA_KV_HEADS * SWA_HEAD_DIM
OFF_RQ = 0
OFF_RK = OFF_RQ + RET_QK_WIDTH
OFF_RV = OFF_RK + RET_QK_WIDTH
OFF_RG = OFF_RV + RET_V_WIDTH
OFF_SQ = OFF_RG + RET_V_WIDTH
OFF_SK = OFF_SQ + SWA_Q_WIDTH
OFF_SV = OFF_SK + SWA_KV_WIDTH
OFF_GR = OFF_SV + SWA_KV_WIDTH
OFF_GS = OFF_GR + D_MODEL
IN_COLS = OFF_GS + D_MODEL

LANES = 128
ROPE_COLS = 5 * LANES

TOKEN_TILE = 512
MERGE_TILE = 1024
RET_TILE = 1024
SWA_TILE = 2048
FF_CHUNK = 256
FFN_DOWN_PARTS = 4
PROJ_SEGMENTS = 8
NORM_ROWS = 32
RET_BLOCK = 256
ADA_COLS = 1152
VMEM_LIMIT = 56 * 1024 * 1024


def _sigmoid(v):
    return 1.0 / (1.0 + jnp.exp(-v))


def _dot(a, b):
    return jnp.dot(a, b, preferred_element_type=F32)


def _dot_nt(a, b):
    return lax.dot_general(a, b, (((1,), (1,)), ((), ())), preferred_element_type=F32)


def _resident(shape):
    zeros = (0,) * len(shape)
    return pl.BlockSpec(shape, lambda *_: zeros, pipeline_mode=pl.Buffered(1))


def _layer_resident(layer, shape):
    zeros = (0,) * len(shape)
    return pl.BlockSpec((None,) + tuple(shape), lambda *_: (layer,) + zeros,
                        pipeline_mode=pl.Buffered(1))


def _wdot(a, w):
    return jnp.dot(a, w.astype(BF16), preferred_element_type=F32)


def _params():
    return pltpu.CompilerParams(dimension_semantics=("arbitrary",), vmem_limit_bytes=VMEM_LIMIT)


def _params2():
    return pltpu.CompilerParams(dimension_semantics=("arbitrary", "arbitrary"),
                                vmem_limit_bytes=VMEM_LIMIT)


def _norm_mod(x, gain, shift, scale):
    y = x * lax.rsqrt(jnp.mean(x * x, axis=-1, keepdims=True) + NORM_EPS)
    return (y * gain) * (1.0 + scale) + shift


def _ada_kernel(c_ref, w_ref, b_ref, o_ref):
    c = c_ref[...]
    act = (c * _sigmoid(c)).astype(BF16)
    o_ref[...] = _dot(act, w_ref[...].astype(BF16)) + b_ref[...]


def _rope_kernel(pos_ref, invf_ref, o_ref):
    assert 2 * RET_FREQS == LANES and 2 * SWA_FREQS == SWA_HEAD_DIM
    half = pos_ref.shape[0] // 2
    lane = lax.broadcasted_iota(jnp.int32, (half, LANES), 1)
    low = lane < RET_FREQS
    pos = jnp.where(low, pos_ref[0:half, :], pos_ref[half:, :])
    ang = pos * invf_ref[...]
    c = jnp.cos(ang)
    s = jnp.sin(ang)
    c_other = pltpu.roll(c, RET_FREQS, 1)
    s_other = pltpu.roll(s, RET_FREQS, 1)
    first_half = (lane & (SWA_HEAD_DIM - 1)) < SWA_FREQS
    even = 2 * (lane & (SWA_FREQS - 1))
    for t in range(2):
        rows = slice(t * half, (t + 1) * half)
        mine = low if t == 0 else jnp.logical_not(low)
        o_ref[rows, 0:LANES] = jnp.where(mine, c, c_other)
        o_ref[rows, LANES:2 * LANES] = jnp.where(low, -1.0, 1.0) * jnp.where(mine, s, s_other)
        cs = jnp.take_along_axis(c, even + RET_FREQS * t, axis=1)
        ss = jnp.take_along_axis(s, even + RET_FREQS * t, axis=1)
        o_ref[rows, 2 * LANES:3 * LANES] = cs
        o_ref[rows, 3 * LANES:4 * LANES] = jnp.where(first_half, -ss, 0.0)
        o_ref[rows, 4 * LANES:5 * LANES] = jnp.where(first_half, 0.0, ss)


def _prep_kernel(c_ref, w_ref, b_ref, pos_ref, invf_ref, mod_ref, rope_ref):
    _ada_kernel(c_ref, w_ref, b_ref, mod_ref)
    _rope_kernel(pos_ref, invf_ref, rope_ref)


def _prep(c_pad, w_ada, b_ada, positions):
    rows = c_pad.shape[0]
    cols = N_MOD * D_MODEL
    blocks = cols // ADA_COLS
    steps = DEPTH * blocks
    n = positions.size
    assert n % steps == 0
    tm = n // steps
    inv_r = 1.0 / (ROPE_THETA ** (jnp.arange(0, RET_DK, 2, dtype=F32) / RET_DK))
    invf = jnp.concatenate([inv_r, inv_r]).reshape(1, LANES)
    pos = positions.astype(F32).reshape(n, 1)
    return pl.pallas_call(
        _prep_kernel,
        grid=(steps,),
        in_specs=[
            pl.BlockSpec((rows, D_MODEL), lambda i: (0, 0)),
            pl.BlockSpec((None, D_MODEL, ADA_COLS), lambda i: (i // blocks, 0, i % blocks)),
            pl.BlockSpec((None, 1, ADA_COLS), lambda i: (i // blocks, 0, i % blocks)),
            pl.BlockSpec((tm, 1), lambda i: (i, 0)),
            pl.BlockSpec((1, LANES), lambda i: (0, 0)),
        ],
        out_specs=[pl.BlockSpec((None, rows, ADA_COLS), lambda i: (i // blocks, 0, i % blocks)),
                   pl.BlockSpec((tm, ROPE_COLS), lambda i: (i, 0))],
        out_shape=[jax.ShapeDtypeStruct((DEPTH, rows, cols), F32),
                   jax.ShapeDtypeStruct((n, ROPE_COLS), F32)],
        compiler_params=_params(),
        name="prep",
    )(c_pad, w_ada, b_ada.reshape(DEPTH, 1, cols), pos, invf)


def _ffn_kernel(layer, mod_row, final, x_ref, xn_ref, gain_ref, mod_ref, modn_ref, wup_hbm, wdn_hbm, *rest):
    if final:
        fgain_ref, o_ref, act_ref, h_ref, wup_ref, wdn_ref, sem = rest
    else:
        o_ref, act_ref, h_ref, wup_ref, wdn_ref, sem = rest
    n_up = D_FF // FF_CHUNK
    dn_rows = D_FF // FFN_DOWN_PARTS

    def up_copy(j, half):
        cols = pl.ds(half * D_FF + j * FF_CHUNK, FF_CHUNK)
        return pltpu.make_async_copy(wup_hbm.at[layer, :, cols], wup_ref.at[:, cols],
                                     sem.at[2 * j + half])

    def down_copy(p):
        rows = pl.ds(p * dn_rows, dn_rows)
        return pltpu.make_async_copy(wdn_hbm.at[layer, rows, :], wdn_ref.at[rows, :],
                                     sem.at[2 * n_up + p])

    def normed(xr, mr):
        return _norm_mod(xr[...], gain_ref[...], mr[mod_row:mod_row + 1, :],
                         mr[mod_row + 1:mod_row + 2, :]).astype(BF16)

    def up_chunk(j):
        lo = j * FF_CHUNK
        g = _wdot(h_ref[...], wup_ref[:, lo:lo + FF_CHUNK])
        u = _wdot(h_ref[...], wup_ref[:, D_FF + lo:D_FF + lo + FF_CHUNK])
        act_ref[:, lo:lo + FF_CHUNK] = (g * _sigmoid(g) * u).astype(BF16)

    def finish_tile():
        y = _wdot(act_ref[...], wdn_ref[...])
        h_ref[...] = normed(xn_ref, modn_ref)
        out = x_ref[...] + (0.5 * mod_ref[mod_row + 2:mod_row + 3, :]) * y
        if final:
            out = out * lax.rsqrt(jnp.mean(out * out, axis=-1, keepdims=True) + NORM_EPS)
            out = out * fgain_ref[...]
        o_ref[...] = out
        up_chunk(0)

    @pl.when(pl.program_id(0) == 0)
    def _():
        for j in range(n_up):
            up_copy(j, 0).start()
            up_copy(j, 1).start()
        for p in range(FFN_DOWN_PARTS):
            down_copy(p).start()
        h_ref[...] = normed(x_ref, mod_ref)
        for j in range(n_up):
            up_copy(j, 0).wait()
            up_copy(j, 1).wait()
            up_chunk(j)
        for p in range(FFN_DOWN_PARTS):
            down_copy(p).wait()
        finish_tile()

    @pl.when(pl.program_id(0) > 0)
    def _():
        for j in range(1, n_up):
            up_chunk(j)
        finish_tile()


def _tile_specs(n, tm, batch, layer):
    tiles = n // tm
    tiles_per_batch = tiles // batch
    nxt = lambda i: jnp.minimum(i + 1, tiles - 1)
    return [
        pl.BlockSpec((tm, D_MODEL), lambda i: (i, 0)),
        pl.BlockSpec((tm, D_MODEL), lambda i: (nxt(i), 0)),
        _resident((1, D_MODEL)),
        pl.BlockSpec((None, N_MOD, D_MODEL), lambda i: (layer * batch + i // tiles_per_batch, 0, 0)),
        pl.BlockSpec((None, N_MOD, D_MODEL), lambda i: (layer * batch + nxt(i) // tiles_per_batch, 0, 0)),
    ]


def _ffn(x, gain, mod, layer, mod_row, w_up, w_down, batch, final_gain=None):
    n = x.shape[0]
    tm = TOKEN_TILE
    final = final_gain is not None
    in_specs = _tile_specs(n, tm, batch, layer) + [
        pl.BlockSpec(memory_space=pl.ANY),
        pl.BlockSpec(memory_space=pl.ANY),
    ]
    args = [x, x, gain.reshape(1, D_MODEL), mod, mod, w_up, w_down]
    if final:
        in_specs.append(_resident((1, D_MODEL)))
        args.append(final_gain.reshape(1, D_MODEL))
    return pl.pallas_call(
        functools.partial(_ffn_kernel, layer, mod_row, final),
        grid=(n // tm,),
        in_specs=in_specs,
        out_specs=pl.BlockSpec((tm, D_MODEL), lambda i: (i, 0)),
        out_shape=jax.ShapeDtypeStruct((n, D_MODEL), F32),
        scratch_shapes=[pltpu.VMEM((tm, D_FF), BF16), pltpu.VMEM((tm, D_MODEL), BF16),
                        pltpu.VMEM((D_MODEL, 2 * D_FF), F32), pltpu.VMEM((D_FF, D_MODEL), F32),
                        pltpu.SemaphoreType.DMA((2 * (D_FF // FF_CHUNK) + FFN_DOWN_PARTS,))],
        compiler_params=_params(),
        name="ffn",
    )(*args)


def _ret_tables():
    h = np.arange(RET_HEADS, dtype=np.float64)
    log_gamma = np.log1p(-(2.0 ** (-5.0 - h)))
    idx = np.arange(RET_BLOCK, dtype=np.float64)
    rel = idx[:, None] - idx[None, :]
    decay = np.where(rel[None] >= 0, np.exp(np.maximum(rel, 0.0)[None] * log_gamma[:, None, None]), 0.0)
    decay = decay * (RET_DK ** -0.5)
    zeta = np.exp((RET_BLOCK - 1 - idx)[None, :] * log_gamma[:, None])
    xi = np.exp((idx + 1)[None, :] * log_gamma[:, None]) * (RET_DK ** -0.5)
    zeta = np.broadcast_to(zeta[:, :, None], (RET_HEADS, RET_BLOCK, RET_DK))
    xi = np.broadcast_to(xi[:, :, None], (RET_HEADS, RET_BLOCK, RET_DK))
    gamma_block = tuple(float(v) for v in np.exp(RET_BLOCK * log_gamma))
    return (jnp.asarray(decay, F32), jnp.asarray(zeta, F32), jnp.asarray(xi, F32), gamma_block)


def _retention_tile(first, gamma_block, q_ref, k_ref, v_ref, g_ref, dec_ref, zeta_ref, xi_ref,
                    o_ref, state_ref, raw_ref):
    rows_total = q_ref.shape[0]
    for c in range(rows_total // RET_BLOCK):
        rows = slice(c * RET_BLOCK, (c + 1) * RET_BLOCK)
        for h in range(RET_HEADS):
            kcols = slice(h * RET_DK, (h + 1) * RET_DK)
            vcols = slice(h * RET_DV, (h + 1) * RET_DV)
            q = q_ref[rows, kcols]
            k = k_ref[rows, kcols]
            v = v_ref[rows, vcols]
            state = state_ref[h]
            if c == 0:
                state = jnp.where(first, 0.0, state)
            scores = _dot_nt(q, k) * dec_ref[h]
            q_in = q.astype(F32) * xi_ref[h]
            kz = (k.astype(F32) * zeta_ref[h]).astype(BF16)
            lhs = jnp.concatenate(
                [jnp.concatenate([scores.astype(BF16), q_in.astype(BF16)], axis=1),
                 jnp.concatenate([kz.T, jnp.zeros((RET_DK, RET_DK), BF16)], axis=1)], axis=0)
            rhs = jnp.concatenate([v, state.astype(BF16)], axis=0)
            res = _dot(lhs, rhs)
            raw_ref[rows, vcols] = res[:RET_BLOCK]
            state_ref[h] = state * gamma_block[h] + res[RET_BLOCK:]

    for r in range(rows_total // NORM_ROWS):
        rows = slice(r * NORM_ROWS, (r + 1) * NORM_ROWS)
        for h in range(RET_HEADS):
            vcols = slice(h * RET_DV, (h + 1) * RET_DV)
            out = raw_ref[rows, vcols]
            mu = jnp.mean(out, axis=-1, keepdims=True)
            dev = out - mu
            var = jnp.mean(dev * dev, axis=-1, keepdims=True)
            gate = g_ref[rows, vcols].astype(F32)
            o_ref[rows, vcols] = (gate * _sigmoid(gate) * (dev * lax.rsqrt(var + NORM_EPS))).astype(BF16)


def _swa_tile(first, sink_ref, q_ref, kv_ref, kvp_ref, o_ref):
    low = lax.broadcasted_iota(jnp.int32, (CHUNK, LANES), 1) < SWA_HEAD_DIM
    qi = lax.broadcasted_iota(jnp.int32, (CHUNK, CHUNK), 0)
    kj = lax.broadcasted_iota(jnp.int32, (CHUNK, CHUNK), 1)
    tri = kj <= qi
    zero = jnp.zeros((CHUNK, LANES), F32)
    ones_lo = jnp.where(low, 1.0, 0.0)
    ones_hi = jnp.where(low, 0.0, 1.0)

    def pieces(kvb):
        kf = kvb[:, :LANES].astype(F32)
        vf = kvb[:, LANES:].astype(F32)
        ks = pltpu.roll(kf, SWA_HEAD_DIM, 1)
        vs = pltpu.roll(vf, SWA_HEAD_DIM, 1)
        out = []
        for lo_src_k, hi_src_k, lo_src_v, hi_src_v in ((kf, ks, vf, vs), (ks, kf, vs, vf)):
            k_lo = jnp.where(low, lo_src_k, zero).astype(BF16)
            k_hi = jnp.where(low, zero, hi_src_k).astype(BF16)
            vo_lo = jnp.concatenate([jnp.where(low, lo_src_v, zero), ones_lo], axis=1).astype(BF16)
            vo_hi = jnp.concatenate([jnp.where(low, zero, hi_src_v), ones_hi], axis=1).astype(BF16)
            out.append((k_lo, k_hi, vo_lo, vo_hi))
        return out

    prev = pieces(kvp_ref[...])
    for j in range(q_ref.shape[0] // CHUNK):
        rows = slice(j * CHUNK, (j + 1) * CHUNK)
        cur = pieces(kv_ref[rows, :])
        kb = [jnp.concatenate([prev[g][0], cur[g][0], prev[g][1], cur[g][1]], axis=0)
              for g in range(SWA_KV_HEADS)]
        vo = [jnp.concatenate([prev[g][2], cur[g][2], prev[g][3], cur[g][3]], axis=0)
              for g in range(SWA_KV_HEADS)]
        pairs_per_group = SWA_Q_HEADS // 2 // SWA_KV_HEADS
        for g in range(SWA_KV_HEADS):
            pairs = range(g * pairs_per_group, (g + 1) * pairs_per_group)
            q_all = jnp.concatenate([q_ref[rows, p * LANES:(p + 1) * LANES] for p in pairs], axis=0)
            s_all = _dot_nt(q_all, kb[g])
            probs_all = []
            sink_all = []
            for i, p in enumerate(pairs):
                s = s_all[i * CHUNK:(i + 1) * CHUNK]
                probs = []
                sink_terms = []
                for hh in range(2):
                    sink = sink_ref[2 * p + hh] * LOG2_E
                    s_prev = s[:, (2 * hh) * CHUNK:(2 * hh + 1) * CHUNK]
                    s_cur = s[:, (2 * hh + 1) * CHUNK:(2 * hh + 2) * CHUNK]
                    if j == 0:
                        s_prev = jnp.where(first, -jnp.inf, s_prev)
                    comb = jnp.where(tri, s_cur, s_prev)
                    m = jnp.maximum(jnp.max(comb, axis=-1, keepdims=True), sink)
                    pe = jnp.exp2(comb - m)
                    probs += [jnp.where(tri, 0.0, pe), jnp.where(tri, pe, 0.0)]
                    sink_terms.append(jnp.exp2(sink - m))
                probs_all.append(jnp.concatenate(probs, axis=1).astype(BF16))
                sink_all.append(jnp.where(low, sink_terms[0], sink_terms[1]))
            res_all = _dot(jnp.concatenate(probs_all, axis=0), vo[g])
            for i, p in enumerate(pairs):
                res = res_all[i * CHUNK:(i + 1) * CHUNK]
                den = res[:, LANES:] + sink_all[i]
                o_ref[rows, p * LANES:(p + 1) * LANES] = (res[:, :LANES] / den).astype(BF16)
        prev = cur


def _proj_kernel(layer, x_ref, xn_ref, gain_ref, mod_ref, modn_ref, rope_ref, w_hbm,
                 rq_ref, rk_ref, rv_ref, rg_ref, sq_ref, skv_ref, gr_ref, gs_ref,
                 h_ref, carry_ref, w_ref, sem):
    def normed(xr, mr):
        return _norm_mod(xr[...], gain_ref[...], mr[3:4, :], mr[4:5, :]).astype(BF16)

    cos_r = rope_ref[:, 0:LANES]
    sin_r = rope_ref[:, LANES:2 * LANES]
    cos_s = rope_ref[:, 2 * LANES:3 * LANES]
    sin_a = rope_ref[:, 3 * LANES:4 * LANES]
    sin_b = rope_ref[:, 4 * LANES:5 * LANES]

    def rope_ret(v):
        return v * cos_r + pltpu.roll(v, RET_FREQS, 1) * sin_r

    def rope_swa(v):
        return (v * cos_s + pltpu.roll(v, LANES - SWA_FREQS, 1) * sin_a
                + pltpu.roll(v, SWA_FREQS, 1) * sin_b)

    def rope_swa_q(v):
        return rope_swa(v) * (SWA_HEAD_DIM ** -0.5 * LOG2_E)

    def project(off, width, out_ref, out_off=0, rope=None):
        y = _wdot(h_ref[...], w_ref[:, off:off + width])
        if rope is None:
            out_ref[:, out_off:out_off + width] = y.astype(out_ref.dtype)
        else:
            for s in range(width // LANES):
                slab = y[:, s * LANES:(s + 1) * LANES]
                out_ref[:, out_off + s * LANES:out_off + (s + 1) * LANES] = rope(slab).astype(out_ref.dtype)

    def seg_skv():
        project(OFF_SK, SWA_KV_WIDTH, skv_ref, rope=rope_swa)
        project(OFF_SV, SWA_KV_WIDTH, skv_ref, out_off=SWA_KV_WIDTH)

    segments = (
        (OFF_RV, RET_V_WIDTH, None),
        (OFF_RQ, RET_QK_WIDTH, lambda: project(OFF_RQ, RET_QK_WIDTH, rq_ref, rope=rope_ret)),
        (OFF_RK, RET_QK_WIDTH, lambda: project(OFF_RK, RET_QK_WIDTH, rk_ref, rope=rope_ret)),
        (OFF_SQ, SWA_Q_WIDTH, lambda: project(OFF_SQ, SWA_Q_WIDTH, sq_ref, rope=rope_swa_q)),
        (OFF_SK, 2 * SWA_KV_WIDTH, seg_skv),
        (OFF_RG, RET_V_WIDTH, lambda: project(OFF_RG, RET_V_WIDTH, rg_ref)),
        (OFF_GR, D_MODEL, lambda: project(OFF_GR, D_MODEL, gr_ref)),
        (OFF_GS, D_MODEL, lambda: project(OFF_GS, D_MODEL, gs_ref)),
    )
    assert len(segments) == PROJ_SEGMENTS

    def w_copy(s):
        cols = pl.ds(segments[s][0], segments[s][1])
        return pltpu.make_async_copy(w_hbm.at[layer, :, cols], w_ref.at[:, cols], sem.at[s])

    def tile_body(wait):
        rv_ref[...] = carry_ref[...]
        for s in range(1, len(segments)):
            if wait:
                w_copy(s).wait()
            segments[s][2]()
        h_ref[...] = normed(xn_ref, modn_ref)
        project(OFF_RV, RET_V_WIDTH, carry_ref)

    @pl.when(pl.program_id(0) == 0)
    def _():
        for s in range(len(segments)):
            w_copy(s).start()
        h_ref[...] = normed(x_ref, mod_ref)
        w_copy(0).wait()
        project(OFF_RV, RET_V_WIDTH, carry_ref)
        tile_body(wait=True)

    @pl.when(pl.program_id(0) > 0)
    def _():
        tile_body(wait=False)


def _proj(x, gain, mod, layer, rope, w_in, batch):
    n = x.shape[0]
    tm = TOKEN_TILE
    widths = (RET_QK_WIDTH, RET_QK_WIDTH, RET_V_WIDTH, RET_V_WIDTH, SWA_Q_WIDTH,
              2 * SWA_KV_WIDTH, D_MODEL, D_MODEL)
    return pl.pallas_call(
        functools.partial(_proj_kernel, layer),
        grid=(n // tm,),
        in_specs=_tile_specs(n, tm, batch, layer) + [
            pl.BlockSpec((tm, ROPE_COLS), lambda i: (i, 0)),
            pl.BlockSpec(memory_space=pl.ANY),
        ],
        out_specs=[pl.BlockSpec((tm, w), lambda i: (i, 0)) for w in widths],
        out_shape=[jax.ShapeDtypeStruct((n, w), BF16) for w in widths],
        scratch_shapes=[pltpu.VMEM((tm, D_MODEL), BF16), pltpu.VMEM((tm, RET_V_WIDTH), BF16),
                        pltpu.VMEM((D_MODEL, IN_COLS), F32), pltpu.SemaphoreType.DMA((PROJ_SEGMENTS,))],
        compiler_params=_params(),
        name="proj",
    )(x, x, gain.reshape(1, D_MODEL), mod, mod, rope, w_in)


def _ret_kernel(gamma_block, q_ref, k_ref, v_ref, g_ref, dec_ref, zeta_ref, xi_ref, o_ref, state_ref,
                raw_ref):
    _retention_tile(pl.program_id(1) == 0, gamma_block, q_ref, k_ref, v_ref, g_ref, dec_ref, zeta_ref,
                    xi_ref, o_ref, state_ref, raw_ref)


def _retention(rq, rk, rv, rg, batch):
    n = rq.shape[0]
    steps = n // batch // RET_TILE
    decay, zeta, xi, gamma_block = _ret_tables()
    row_block = lambda w: pl.BlockSpec((RET_TILE, w), lambda b, t: (b * steps + t, 0))
    return pl.pallas_call(
        functools.partial(_ret_kernel, gamma_block),
        grid=(batch, steps),
        in_specs=[row_block(RET_QK_WIDTH), row_block(RET_QK_WIDTH), row_block(RET_V_WIDTH),
                  row_block(RET_V_WIDTH),
                  _resident((RET_HEADS, RET_BLOCK, RET_BLOCK)), _resident((RET_HEADS, RET_BLOCK, RET_DK)),
                  _resident((RET_HEADS, RET_BLOCK, RET_DK))],
        out_specs=row_block(RET_V_WIDTH),
        out_shape=jax.ShapeDtypeStruct((n, RET_V_WIDTH), BF16),
        scratch_shapes=[pltpu.VMEM((RET_HEADS, RET_DK, RET_DV), F32),
                        pltpu.VMEM((RET_TILE, RET_V_WIDTH), F32)],
        compiler_params=_params2(),
        name="retention",
    )(rq, rk, rv, rg, decay, zeta, xi)


def _swa_kernel(sink_ref, q_ref, kv_ref, kvp_ref, o_ref):
    _swa_tile(pl.program_id(1) == 0, sink_ref, q_ref, kv_ref, kvp_ref, o_ref)


def _swa(sinks, sq, skv, batch):
    n = sq.shape[0]
    steps = n // batch // SWA_TILE
    per = SWA_TILE // CHUNK
    return pl.pallas_call(
        _swa_kernel,
        grid=(batch, steps),
        in_specs=[
            pl.BlockSpec(memory_space=pltpu.SMEM),
            pl.BlockSpec((SWA_TILE, SWA_Q_WIDTH), lambda b, t: (b * steps + t, 0)),
            pl.BlockSpec((SWA_TILE, 2 * SWA_KV_WIDTH), lambda b, t: (b * steps + t, 0)),
            pl.BlockSpec((CHUNK, 2 * SWA_KV_WIDTH),
                         lambda b, t: ((b * steps + t) * per - jnp.minimum(t, 1), 0)),
        ],
        out_specs=pl.BlockSpec((SWA_TILE, SWA_Q_WIDTH), lambda b, t: (b * steps + t, 0)),
        out_shape=jax.ShapeDtypeStruct((n, SWA_Q_WIDTH), BF16),
        compiler_params=_params2(),
        name="swa",
    )(sinks, sq, skv, skv)


def _merge_kernel(x_ref, mod_ref, ret_ref, swa_ref, gr_ref, gs_ref, wr_ref, ws_ref, wo_ref, o_ref):
    br = _wdot(ret_ref[...], wr_ref[...])
    bs = _wdot(swa_ref[...], ws_ref[...])
    merged = _sigmoid(gr_ref[...].astype(F32)) * br + _sigmoid(gs_ref[...].astype(F32)) * bs
    y = _wdot(merged.astype(BF16), wo_ref[...])
    o_ref[...] = x_ref[...] + mod_ref[5:6, :] * y


def _merge(x, mod, layer, ret, swa, gr, gs, w_ret, w_swa, w_out, batch):
    n = x.shape[0]
    tm = MERGE_TILE
    tiles_per_batch = n // batch // tm
    rows = lambda w: pl.BlockSpec((tm, w), lambda i: (i, 0))
    return pl.pallas_call(
        _merge_kernel,
        grid=(n // tm,),
        in_specs=[
            rows(D_MODEL),
            pl.BlockSpec((None, N_MOD, D_MODEL), lambda i: (layer * batch + i // tiles_per_batch, 0, 0)),
            rows(RET_V_WIDTH), rows(SWA_Q_WIDTH), rows(D_MODEL), rows(D_MODEL),
            _layer_resident(layer, (RET_V_WIDTH, D_MODEL)), _layer_resident(layer, (SWA_Q_WIDTH, D_MODEL)),
            _layer_resident(layer, (D_MODEL, D_MODEL)),
        ],
        out_specs=rows(D_MODEL),
        out_shape=jax.ShapeDtypeStruct((n, D_MODEL), F32),
        compiler_params=_params(),
        name="merge",
    )(x, mod, ret, swa, gr, gs, w_ret, w_swa, w_out)


def kernel(x, c, positions, norm_ffn1, norm_mix, norm_ffn2, final_norm, w_ada, b_ada,
           ffn1_w_up, ffn1_w_down, ffn2_w_up, ffn2_w_down, w_in, sinks,
           w_branch_ret, w_branch_swa, w_out):
    batch, seq, d = x.shape
    n = batch * seq
    assert d == D_MODEL and seq % max(TOKEN_TILE, MERGE_TILE, RET_TILE, SWA_TILE) == 0
    xf = x.reshape(n, d)

    pad_rows = -batch % 8
    c_pad = jnp.pad(c, ((0, pad_rows), (0, 0)))
    mod, rope = _prep(c_pad, w_ada, b_ada, positions)
    mod = mod[:, :batch].reshape(DEPTH * batch, N_MOD, D_MODEL)

    for l in range(DEPTH):
        xf = _ffn(xf, norm_ffn1[l], mod, l, 0, ffn1_w_up, ffn1_w_down, batch)
        rq, rk, rv, rg, sq, skv, gr, gs = _proj(xf, norm_mix[l], mod, l, rope, w_in, batch)
        ret = _retention(rq, rk, rv, rg, batch)
        swa = _swa(sinks[l], sq, skv, batch)
        xf = _merge(xf, mod, l, ret, swa, gr, gs, w_branch_ret, w_branch_swa, w_out, batch)
        final_gain = final_norm if l == DEPTH - 1 else None
        xf = _ffn(xf, norm_ffn2[l], mod, l, 6, ffn2_w_up, ffn2_w_down, batch, final_gain)
    return xf.reshape(batch, seq, d)
```

```python
import functools

import numpy as np
import jax
import jax.numpy as jnp
from jax import lax
from jax.experimental import pallas as pl
from jax.experimental.pallas import tpu as pltpu

F32 = jnp.float32
BF16 = jnp.bfloat16

D_MODEL = 1024
DEPTH = 2
RET_HEADS = 4
RET_DK = 128
RET_DV = 256
CHUNK = 128
SWA_Q_HEADS = 16
SWA_KV_HEADS = 2
SWA_HEAD_DIM = 64
D_FF = 2816
ROPE_THETA = 10000.0
NORM_EPS = 1e-6
N_MOD = 9
LOG2_E = 1.4426950408889634

RET_FREQS = RET_DK // 2
SWA_FREQS = SWA_HEAD_DIM // 2
RET_QK_WIDTH = RET_HEADS * RET_DK
RET_V_WIDTH = RET_HEADS * RET_DV
SWA_Q_WIDTH = SWA_Q_HEADS * SWA_HEAD_DIM
SWA_KV_WIDTH = SWA_KV_HEADS * SWA_HEAD_DIM
OFF_RQ = 0
OFF_RK = OFF_RQ + RET_QK_WIDTH
OFF_RV = OFF_RK + RET_QK_WIDTH
OFF_RG = OFF_RV + RET_V_WIDTH
OFF_SQ = OFF_RG + RET_V_WIDTH
OFF_SK = OFF_SQ + SWA_Q_WIDTH
OFF_SV = OFF_SK + SWA_KV_WIDTH
OFF_GR = OFF_SV + SWA_KV_WIDTH
OFF_GS = OFF_GR + D_MODEL
IN_COLS = OFF_GS + D_MODEL

LANES = 128
ROPE_COLS = 5 * LANES

TOKEN_TILE = 512
MERGE_TILE = 1024
RET_TILE = 1024
SWA_TILE = 2048
FF_CHUNK = 256
FFN_DOWN_PARTS = 4
PROJ_SEGMENTS = 8
NORM_ROWS = 32
RET_BLOCK = 256
ADA_COLS = 2304
VMEM_LIMIT = 56 * 1024 * 1024


def _sigmoid(v):
    return 1.0 / (1.0 + jnp.exp(-v))


def _dot(a, b):
    return jnp.dot(a, b, preferred_element_type=F32)


def _dot_nt(a, b):
    return lax.dot_general(a, b, (((1,), (1,)), ((), ())), preferred_element_type=F32)


def _resident(shape):
    zeros = (0,) * len(shape)
    return pl.BlockSpec(shape, lambda *_: zeros, pipeline_mode=pl.Buffered(1))


def _wdot(a, w):
    return jnp.dot(a, w.astype(BF16), preferred_element_type=F32)


def _params():
    return pltpu.CompilerParams(dimension_semantics=("arbitrary",), vmem_limit_bytes=VMEM_LIMIT)


def _params2():
    return pltpu.CompilerParams(dimension_semantics=("arbitrary", "arbitrary"),
                                vmem_limit_bytes=VMEM_LIMIT)


def _norm_mod(x, gain, shift, scale):
    y = x * lax.rsqrt(jnp.mean(x * x, axis=-1, keepdims=True) + NORM_EPS)
    return (y * gain) * (1.0 + scale) + shift


def _ada_kernel(c_ref, w_ref, b_ref, o_ref):
    c = c_ref[...]
    act = (c * _sigmoid(c)).astype(BF16)
    o_ref[...] = _dot(act, w_ref[...].astype(BF16)) + b_ref[...]


def _rope_kernel(pos_ref, invf_ref, o_ref):
    assert 2 * RET_FREQS == LANES and 2 * SWA_FREQS == SWA_HEAD_DIM
    half = pos_ref.shape[0] // 2
    lane = lax.broadcasted_iota(jnp.int32, (half, LANES), 1)
    low = lane < RET_FREQS
    pos = jnp.where(low, pos_ref[0:half, :], pos_ref[half:, :])
    ang = pos * invf_ref[...]
    c = jnp.cos(ang)
    s = jnp.sin(ang)
    c_other = pltpu.roll(c, RET_FREQS, 1)
    s_other = pltpu.roll(s, RET_FREQS, 1)
    first_half = (lane & (SWA_HEAD_DIM - 1)) < SWA_FREQS
    even = 2 * (lane & (SWA_FREQS - 1))
    for t in range(2):
        rows = slice(t * half, (t + 1) * half)
        mine = low if t == 0 else jnp.logical_not(low)
        o_ref[rows, 0:LANES] = jnp.where(mine, c, c_other)
        o_ref[rows, LANES:2 * LANES] = jnp.where(low, -1.0, 1.0) * jnp.where(mine, s, s_other)
        cs = jnp.take_along_axis(c, even + RET_FREQS * t, axis=1)
        ss = jnp.take_along_axis(s, even + RET_FREQS * t, axis=1)
        o_ref[rows, 2 * LANES:3 * LANES] = cs
        o_ref[rows, 3 * LANES:4 * LANES] = jnp.where(first_half, -ss, 0.0)
        o_ref[rows, 4 * LANES:5 * LANES] = jnp.where(first_half, 0.0, ss)


def _prep_kernel(c_ref, w_ref, b_ref, pos_ref, invf_ref, mod_ref, rope_ref):
    _ada_kernel(c_ref, w_ref, b_ref, mod_ref)
    _rope_kernel(pos_ref, invf_ref, rope_ref)


def _prep(c_pad, w_ada, b_ada, positions):
    rows = c_pad.shape[0]
    cols = N_MOD * D_MODEL
    blocks = cols // ADA_COLS
    steps = DEPTH * blocks
    n = positions.size
    assert n % steps == 0
    tm = n // steps
    inv_r = 1.0 / (ROPE_THETA ** (jnp.arange(0, RET_DK, 2, dtype=F32) / RET_DK))
    invf = jnp.concatenate([inv_r, inv_r]).reshape(1, LANES)
    pos = positions.astype(F32).reshape(n, 1)
    return pl.pallas_call(
        _prep_kernel,
        grid=(steps,),
        in_specs=[
            pl.BlockSpec((rows, D_MODEL), lambda i: (0, 0)),
            pl.BlockSpec((None, D_MODEL, ADA_COLS), lambda i: (i // blocks, 0, i % blocks)),
            pl.BlockSpec((None, 1, ADA_COLS), lambda i: (i // blocks, 0, i % blocks)),
            pl.BlockSpec((tm, 1), lambda i: (i, 0)),
            pl.BlockSpec((1, LANES), lambda i: (0, 0)),
        ],
        out_specs=[pl.BlockSpec((None, rows, ADA_COLS), lambda i: (i // blocks, 0, i % blocks)),
                   pl.BlockSpec((tm, ROPE_COLS), lambda i: (i, 0))],
        out_shape=[jax.ShapeDtypeStruct((DEPTH, rows, cols), F32),
                   jax.ShapeDtypeStruct((n, ROPE_COLS), F32)],
        compiler_params=_params(),
        name="prep",
    )(c_pad, w_ada, b_ada.reshape(DEPTH, 1, cols), pos, invf)


def _ffn_kernel(layer, mod_row, final, x_ref, xn_ref, gain_ref, mod_ref, modn_ref, wup_hbm, wdn_hbm, *rest):
    if final:
        fgain_ref, o_ref, act_ref, h_ref, wup_ref, wdn_ref, sem = rest
    else:
        o_ref, act_ref, h_ref, wup_ref, wdn_ref, sem = rest
    n_up = D_FF // FF_CHUNK
    dn_rows = D_FF // FFN_DOWN_PARTS

    def up_copy(j, half):
        cols = pl.ds(half * D_FF + j * FF_CHUNK, FF_CHUNK)
        return pltpu.make_async_copy(wup_hbm.at[layer, :, cols], wup_ref.at[:, cols],
                                     sem.at[2 * j + half])

    def down_copy(p):
        rows = pl.ds(p * dn_rows, dn_rows)
        return pltpu.make_async_copy(wdn_hbm.at[layer, rows, :], wdn_ref.at[rows, :],
                                     sem.at[2 * n_up + p])

    def normed(xr, mr):
        return _norm_mod(xr[...], gain_ref[...], mr[mod_row:mod_row + 1, :],
                         mr[mod_row + 1:mod_row + 2, :]).astype(BF16)

    def up_chunk(j):
        lo = j * FF_CHUNK
        g = _wdot(h_ref[...], wup_ref[:, lo:lo + FF_CHUNK])
        u = _wdot(h_ref[...], wup_ref[:, D_FF + lo:D_FF + lo + FF_CHUNK])
        act_ref[:, lo:lo + FF_CHUNK] = (g * _sigmoid(g) * u).astype(BF16)

    def finish_tile():
        y = _wdot(act_ref[...], wdn_ref[...])
        h_ref[...] = normed(xn_ref, modn_ref)
        out = x_ref[...] + (0.5 * mod_ref[mod_row + 2:mod_row + 3, :]) * y
        if final:
            out = out * lax.rsqrt(jnp.mean(out * out, axis=-1, keepdims=True) + NORM_EPS)
            out = out * fgain_ref[...]
        o_ref[...] = out
        up_chunk(0)

    @pl.when(pl.program_id(0) == 0)
    def _():
        for j in range(n_up):
            up_copy(j, 0).start()
            up_copy(j, 1).start()
        for p in range(FFN_DOWN_PARTS):
            down_copy(p).start()
        h_ref[...] = normed(x_ref, mod_ref)
        for j in range(n_up):
            up_copy(j, 0).wait()
            up_copy(j, 1).wait()
            up_chunk(j)
        for p in range(FFN_DOWN_PARTS):
            down_copy(p).wait()
        finish_tile()

    @pl.when(pl.program_id(0) > 0)
    def _():
        for j in range(1, n_up):
            up_chunk(j)
        finish_tile()


def _tile_specs(n, tm, batch, layer):
    tiles = n // tm
    tiles_per_batch = tiles // batch
    nxt = lambda i: jnp.minimum(i + 1, tiles - 1)
    return [
        pl.BlockSpec((tm, D_MODEL), lambda i: (i, 0)),
        pl.BlockSpec((tm, D_MODEL), lambda i: (nxt(i), 0)),
        _resident((1, D_MODEL)),
        pl.BlockSpec((None, N_MOD, D_MODEL), lambda i: (layer * batch + i // tiles_per_batch, 0, 0)),
        pl.BlockSpec((None, N_MOD, D_MODEL), lambda i: (layer * batch + nxt(i) // tiles_per_batch, 0, 0)),
    ]


def _ffn(x, gain, mod, layer, mod_row, w_up, w_down, batch, final_gain=None):
    n = x.shape[0]
    tm = TOKEN_TILE
    final = final_gain is not None
    in_specs = _tile_specs(n, tm, batch, layer) + [
        pl.BlockSpec(memory_space=pl.ANY),
        pl.BlockSpec(memory_space=pl.ANY),
    ]
    args = [x, x, gain.reshape(1, D_MODEL), mod, mod, w_up, w_down]
    if final:
        in_specs.append(_resident((1, D_MODEL)))
        args.append(final_gain.reshape(1, D_MODEL))
    return pl.pallas_call(
        functools.partial(_ffn_kernel, layer, mod_row, final),
        grid=(n // tm,),
        in_specs=in_specs,
        out_specs=pl.BlockSpec((tm, D_MODEL), lambda i: (i, 0)),
        out_shape=jax.ShapeDtypeStruct((n, D_MODEL), F32),
        scratch_shapes=[pltpu.VMEM((tm, D_FF), BF16), pltpu.VMEM((tm, D_MODEL), BF16),
                        pltpu.VMEM((D_MODEL, 2 * D_FF), F32), pltpu.VMEM((D_FF, D_MODEL), F32),
                        pltpu.SemaphoreType.DMA((2 * (D_FF // FF_CHUNK) + FFN_DOWN_PARTS,))],
        compiler_params=_params(),
        name="ffn",
    )(*args)


def _ret_tables():
    h = np.arange(RET_HEADS, dtype=np.float64)
    log_gamma = np.log1p(-(2.0 ** (-5.0 - h)))
    idx = np.arange(RET_BLOCK, dtype=np.float64)
    rel = idx[:, None] - idx[None, :]
    decay = np.where(rel[None] >= 0, np.exp(np.maximum(rel, 0.0)[None] * log_gamma[:, None, None]), 0.0)
    decay = decay * (RET_DK ** -0.5)
    zeta = np.exp((RET_BLOCK - 1 - idx)[None, :] * log_gamma[:, None])
    xi = np.exp((idx + 1)[None, :] * log_gamma[:, None]) * (RET_DK ** -0.5)
    zeta = np.broadcast_to(zeta[:, :, None], (RET_HEADS, RET_BLOCK, RET_DK))
    xi = np.broadcast_to(xi[:, :, None], (RET_HEADS, RET_BLOCK, RET_DK))
    gamma_block = tuple(float(v) for v in np.exp(RET_BLOCK * log_gamma))
    return (jnp.asarray(decay, F32), jnp.asarray(zeta, F32), jnp.asarray(xi, F32), gamma_block)


def _retention_tile(first, gamma_block, q_ref, k_ref, v_ref, g_ref, dec_ref, zeta_ref, xi_ref,
                    o_ref, state_ref, raw_ref):
    rows_total = q_ref.shape[0]
    for c in range(rows_total // RET_BLOCK):
        rows = slice(c * RET_BLOCK, (c + 1) * RET_BLOCK)
        for h in range(RET_HEADS):
            kcols = slice(h * RET_DK, (h + 1) * RET_DK)
            vcols = slice(h * RET_DV, (h + 1) * RET_DV)
            q = q_ref[rows, kcols]
            k = k_ref[rows, kcols]
            v = v_ref[rows, vcols]
            state = state_ref[h]
            if c == 0:
                state = jnp.where(first, 0.0, state)
            scores = _dot_nt(q, k) * dec_ref[h]
            q_in = q.astype(F32) * xi_ref[h]
            kz = (k.astype(F32) * zeta_ref[h]).astype(BF16)
            lhs = jnp.concatenate(
                [jnp.concatenate([scores.astype(BF16), q_in.astype(BF16)], axis=1),
                 jnp.concatenate([kz.T, jnp.zeros((RET_DK, RET_DK), BF16)], axis=1)], axis=0)
            rhs = jnp.concatenate([v, state.astype(BF16)], axis=0)
            res = _dot(lhs, rhs)
            raw_ref[rows, vcols] = res[:RET_BLOCK]
            state_ref[h] = state * gamma_block[h] + res[RET_BLOCK:]

    for r in range(rows_total // NORM_ROWS):
        rows = slice(r * NORM_ROWS, (r + 1) * NORM_ROWS)
        for h in range(RET_HEADS):
            vcols = slice(h * RET_DV, (h + 1) * RET_DV)
            out = raw_ref[rows, vcols]
            mu = jnp.mean(out, axis=-1, keepdims=True)
            dev = out - mu
            var = jnp.mean(dev * dev, axis=-1, keepdims=True)
            gate = g_ref[rows, vcols].astype(F32)
            o_ref[rows, vcols] = (gate * _sigmoid(gate) * (dev * lax.rsqrt(var + NORM_EPS))).astype(BF16)


def _swa_tile(first, sink_ref, q_ref, kv_ref, kvp_ref, o_ref):
    low = lax.broadcasted_iota(jnp.int32, (CHUNK, LANES), 1) < SWA_HEAD_DIM
    qi = lax.broadcasted_iota(jnp.int32, (CHUNK, CHUNK), 0)
    kj = lax.broadcasted_iota(jnp.int32, (CHUNK, CHUNK), 1)
    tri = kj <= qi
    zero = jnp.zeros((CHUNK, LANES), F32)
    ones_lo = jnp.where(low, 1.0, 0.0)
    ones_hi = jnp.where(low, 0.0, 1.0)

    def pieces(kvb):
        kf = kvb[:, :LANES].astype(F32)
        vf = kvb[:, LANES:].astype(F32)
        ks = pltpu.roll(kf, SWA_HEAD_DIM, 1)
        vs = pltpu.roll(vf, SWA_HEAD_DIM, 1)
        out = []
        for lo_src_k, hi_src_k, lo_src_v, hi_src_v in ((kf, ks, vf, vs), (ks, kf, vs, vf)):
            k_lo = jnp.where(low, lo_src_k, zero).astype(BF16)
            k_hi = jnp.where(low, zero, hi_src_k).astype(BF16)
            vo_lo = jnp.concatenate([jnp.where(low, lo_src_v, zero), ones_lo], axis=1).astype(BF16)
            vo_hi = jnp.concatenate([jnp.where(low, zero, hi_src_v), ones_hi], axis=1).astype(BF16)
            out.append((k_lo, k_hi, vo_lo, vo_hi))
        return out

    prev = pieces(kvp_ref[...])
    for j in range(q_ref.shape[0] // CHUNK):
        rows = slice(j * CHUNK, (j + 1) * CHUNK)
        cur = pieces(kv_ref[rows, :])
        kb = [jnp.concatenate([prev[g][0], cur[g][0], prev[g][1], cur[g][1]], axis=0)
              for g in range(SWA_KV_HEADS)]
        vo = [jnp.concatenate([prev[g][2], cur[g][2], prev[g][3], cur[g][3]], axis=0)
              for g in range(SWA_KV_HEADS)]
        pairs_per_group = SWA_Q_HEADS // 2 // SWA_KV_HEADS
        for g in range(SWA_KV_HEADS):
            pairs = range(g * pairs_per_group, (g + 1) * pairs_per_group)
            q_all = jnp.concatenate([q_ref[rows, p * LANES:(p + 1) * LANES] for p in pairs], axis=0)
            s_all = _dot_nt(q_all, kb[g])
            probs_all = []
            sink_all = []
            for i, p in enumerate(pairs):
                s = s_all[i * CHUNK:(i + 1) * CHUNK]
                probs = []
                sink_terms = []
                for hh in range(2):
                    sink = sink_ref[2 * p + hh] * LOG2_E
                    s_prev = s[:, (2 * hh) * CHUNK:(2 * hh + 1) * CHUNK]
                    s_cur = s[:, (2 * hh + 1) * CHUNK:(2 * hh + 2) * CHUNK]
                    if j == 0:
                        s_prev = jnp.where(first, -jnp.inf, s_prev)
                    comb = jnp.where(tri, s_cur, s_prev)
                    m = jnp.maximum(jnp.max(comb, axis=-1, keepdims=True), sink)
                    pe = jnp.exp2(comb - m)
                    probs += [jnp.where(tri, 0.0, pe), jnp.where(tri, pe, 0.0)]
                    sink_terms.append(jnp.exp2(sink - m))
                probs_all.append(jnp.concatenate(probs, axis=1).astype(BF16))
                sink_all.append(jnp.where(low, sink_terms[0], sink_terms[1]))
            res_all = _dot(jnp.concatenate(probs_all, axis=0), vo[g])
            for i, p in enumerate(pairs):
                res = res_all[i * CHUNK:(i + 1) * CHUNK]
                den = res[:, LANES:] + sink_all[i]
                o_ref[rows, p * LANES:(p + 1) * LANES] = (res[:, :LANES] / den).astype(BF16)
        prev = cur


def _proj_kernel(layer, x_ref, xn_ref, gain_ref, mod_ref, modn_ref, rope_ref, w_hbm,
                 rq_ref, rk_ref, rv_ref, rg_ref, sq_ref, skv_ref, gr_ref, gs_ref,
                 h_ref, carry_ref, w_ref, sem):
    def normed(xr, mr):
        return _norm_mod(xr[...], gain_ref[...], mr[3:4, :], mr[4:5, :]).astype(BF16)

    cos_r = rope_ref[:, 0:LANES]
    sin_r = rope_ref[:, LANES:2 * LANES]
    cos_s = rope_ref[:, 2 * LANES:3 * LANES]
    sin_a = rope_ref[:, 3 * LANES:4 * LANES]
    sin_b = rope_ref[:, 4 * LANES:5 * LANES]

    def rope_ret(v):
        return v * cos_r + pltpu.roll(v, RET_FREQS, 1) * sin_r

    def rope_swa(v):
        return (v * cos_s + pltpu.roll(v, LANES - SWA_FREQS, 1) * sin_a
                + pltpu.roll(v, SWA_FREQS, 1) * sin_b)

    def rope_swa_q(v):
        return rope_swa(v) * (SWA_HEAD_DIM ** -0.5 * LOG2_E)

    def project(off, width, out_ref, out_off=0, rope=None):
        y = _wdot(h_ref[...], w_ref[:, off:off + width])
        if rope is None:
            out_ref[:, out_off:out_off + width] = y.astype(out_ref.dtype)
        else:
            for s in range(width // LANES):
                slab = y[:, s * LANES:(s + 1) * LANES]
                out_ref[:, out_off + s * LANES:out_off + (s + 1) * LANES] = rope(slab).astype(out_ref.dtype)

    def seg_skv():
        project(OFF_SK, SWA_KV_WIDTH, skv_ref, rope=rope_swa)
        project(OFF_SV, SWA_KV_WIDTH, skv_ref, out_off=SWA_KV_WIDTH)

    segments = (
        (OFF_RV, RET_V_WIDTH, None),
        (OFF_RQ, RET_QK_WIDTH, lambda: project(OFF_RQ, RET_QK_WIDTH, rq_ref, rope=rope_ret)),
        (OFF_RK, RET_QK_WIDTH, lambda: project(OFF_RK, RET_QK_WIDTH, rk_ref, rope=rope_ret)),
        (OFF_SQ, SWA_Q_WIDTH, lambda: project(OFF_SQ, SWA_Q_WIDTH, sq_ref, rope=rope_swa_q)),
        (OFF_SK, 2 * SWA_KV_WIDTH, seg_skv),
        (OFF_RG, RET_V_WIDTH, lambda: project(OFF_RG, RET_V_WIDTH, rg_ref)),
        (OFF_GR, D_MODEL, lambda: project(OFF_GR, D_MODEL, gr_ref)),
        (OFF_GS, D_MODEL, lambda: project(OFF_GS, D_MODEL, gs_ref)),
    )
    assert len(segments) == PROJ_SEGMENTS

    def w_copy(s):
        cols = pl.ds(segments[s][0], segments[s][1])
        return pltpu.make_async_copy(w_hbm.at[layer, :, cols], w_ref.at[:, cols], sem.at[s])

    def tile_body(wait):
        rv_ref[...] = carry_ref[...]
        for s in range(1, len(segments)):
            if wait:
                w_copy(s).wait()
            segments[s][2]()
        h_ref[...] = normed(xn_ref, modn_ref)
        project(OFF_RV, RET_V_WIDTH, carry_ref)

    @pl.when(pl.program_id(0) == 0)
    def _():
        for s in range(len(segments)):
            w_copy(s).start()
        h_ref[...] = normed(x_ref, mod_ref)
        w_copy(0).wait()
        project(OFF_RV, RET_V_WIDTH, carry_ref)
        tile_body(wait=True)

    @pl.when(pl.program_id(0) > 0)
    def _():
        tile_body(wait=False)


def _proj(x, gain, mod, layer, rope, w_in, batch):
    n = x.shape[0]
    tm = TOKEN_TILE
    widths = (RET_QK_WIDTH, RET_QK_WIDTH, RET_V_WIDTH, RET_V_WIDTH, SWA_Q_WIDTH,
              2 * SWA_KV_WIDTH, D_MODEL, D_MODEL)
    return pl.pallas_call(
        functools.partial(_proj_kernel, layer),
        grid=(n // tm,),
        in_specs=_tile_specs(n, tm, batch, layer) + [
            pl.BlockSpec((tm, ROPE_COLS), lambda i: (i, 0)),
            pl.BlockSpec(memory_space=pl.ANY),
        ],
        out_specs=[pl.BlockSpec((tm, w), lambda i: (i, 0)) for w in widths],
        out_shape=[jax.ShapeDtypeStruct((n, w), BF16) for w in widths],
        scratch_shapes=[pltpu.VMEM((tm, D_MODEL), BF16), pltpu.VMEM((tm, RET_V_WIDTH), BF16),
                        pltpu.VMEM((D_MODEL, IN_COLS), F32), pltpu.SemaphoreType.DMA((PROJ_SEGMENTS,))],
        compiler_params=_params(),
        name="proj",
    )(x, x, gain.reshape(1, D_MODEL), mod, mod, rope, w_in)


def _ret_kernel(gamma_block, q_ref, k_ref, v_ref, g_ref, dec_ref, zeta_ref, xi_ref, o_ref, state_ref,
                raw_ref):
    _retention_tile(pl.program_id(1) == 0, gamma_block, q_ref, k_ref, v_ref, g_ref, dec_ref, zeta_ref,
                    xi_ref, o_ref, state_ref, raw_ref)


def _retention(rq, rk, rv, rg, batch):
    n = rq.shape[0]
    steps = n // batch // RET_TILE
    decay, zeta, xi, gamma_block = _ret_tables()
    row_block = lambda w: pl.BlockSpec((RET_TILE, w), lambda b, t: (b * steps + t, 0))
    return pl.pallas_call(
        functools.partial(_ret_kernel, gamma_block),
        grid=(batch, steps),
        in_specs=[row_block(RET_QK_WIDTH), row_block(RET_QK_WIDTH), row_block(RET_V_WIDTH),
                  row_block(RET_V_WIDTH),
                  _resident((RET_HEADS, RET_BLOCK, RET_BLOCK)), _resident((RET_HEADS, RET_BLOCK, RET_DK)),
                  _resident((RET_HEADS, RET_BLOCK, RET_DK))],
        out_specs=row_block(RET_V_WIDTH),
        out_shape=jax.ShapeDtypeStruct((n, RET_V_WIDTH), BF16),
        scratch_shapes=[pltpu.VMEM((RET_HEADS, RET_DK, RET_DV), F32),
                        pltpu.VMEM((RET_TILE, RET_V_WIDTH), F32)],
        compiler_params=_params2(),
        name="retention",
    )(rq, rk, rv, rg, decay, zeta, xi)


def _swa_kernel(sink_ref, q_ref, kv_ref, kvp_ref, o_ref):
    _swa_tile(pl.program_id(1) == 0, sink_ref, q_ref, kv_ref, kvp_ref, o_ref)


def _swa(sinks, sq, skv, batch):
    n = sq.shape[0]
    steps = n // batch // SWA_TILE
    per = SWA_TILE // CHUNK
    return pl.pallas_call(
        _swa_kernel,
        grid=(batch, steps),
        in_specs=[
            pl.BlockSpec(memory_space=pltpu.SMEM),
            pl.BlockSpec((SWA_TILE, SWA_Q_WIDTH), lambda b, t: (b * steps + t, 0)),
            pl.BlockSpec((SWA_TILE, 2 * SWA_KV_WIDTH), lambda b, t: (b * steps + t, 0)),
            pl.BlockSpec((CHUNK, 2 * SWA_KV_WIDTH),
                         lambda b, t: ((b * steps + t) * per - jnp.minimum(t, 1), 0)),
        ],
        out_specs=pl.BlockSpec((SWA_TILE, SWA_Q_WIDTH), lambda b, t: (b * steps + t, 0)),
        out_shape=jax.ShapeDtypeStruct((n, SWA_Q_WIDTH), BF16),
        compiler_params=_params2(),
        name="swa",
    )(sinks, sq, skv, skv)


def _merge_kernel(layer, x_ref, mod_ref, ret_ref, swa_ref, gr_ref, gs_ref, wr_hbm, ws_hbm, wo_hbm,
                  o_ref, wr_ref, ws_ref, wo_ref, sem):
    copies = [pltpu.make_async_copy(src.at[layer], dst, sem.at[i])
              for i, (src, dst) in enumerate(((wr_hbm, wr_ref), (ws_hbm, ws_ref), (wo_hbm, wo_ref)))]

    def tile_body(wait):
        if wait:
            copies[0].wait()
        br = _wdot(ret_ref[...], wr_ref[...])
        if wait:
            copies[1].wait()
        bs = _wdot(swa_ref[...], ws_ref[...])
        merged = _sigmoid(gr_ref[...].astype(F32)) * br + _sigmoid(gs_ref[...].astype(F32)) * bs
        if wait:
            copies[2].wait()
        y = _wdot(merged.astype(BF16), wo_ref[...])
        o_ref[...] = x_ref[...] + mod_ref[5:6, :] * y

    @pl.when(pl.program_id(0) == 0)
    def _():
        for copy in copies:
            copy.start()
        tile_body(wait=True)

    @pl.when(pl.program_id(0) > 0)
    def _():
        tile_body(wait=False)


def _merge(x, mod, layer, ret, swa, gr, gs, w_ret, w_swa, w_out, batch):
    n = x.shape[0]
    tm = MERGE_TILE
    tiles_per_batch = n // batch // tm
    rows = lambda w: pl.BlockSpec((tm, w), lambda i: (i, 0))
    return pl.pallas_call(
        functools.partial(_merge_kernel, layer),
        grid=(n // tm,),
        in_specs=[
            rows(D_MODEL),
            pl.BlockSpec((None, N_MOD, D_MODEL), lambda i: (layer * batch + i // tiles_per_batch, 0, 0)),
            rows(RET_V_WIDTH), rows(SWA_Q_WIDTH), rows(D_MODEL), rows(D_MODEL),
            pl.BlockSpec(memory_space=pl.ANY), pl.BlockSpec(memory_space=pl.ANY),
            pl.BlockSpec(memory_space=pl.ANY),
        ],
        out_specs=rows(D_MODEL),
        out_shape=jax.ShapeDtypeStruct((n, D_MODEL), F32),
        scratch_shapes=[pltpu.VMEM((RET_V_WIDTH, D_MODEL), F32), pltpu.VMEM((SWA_Q_WIDTH, D_MODEL), F32),
                        pltpu.VMEM((D_MODEL, D_MODEL), F32), pltpu.SemaphoreType.DMA((3,))],
        compiler_params=_params(),
        name="merge",
    )(x, mod, ret, swa, gr, gs, w_ret, w_swa, w_out)


def kernel(x, c, positions, norm_ffn1, norm_mix, norm_ffn2, final_norm, w_ada, b_ada,
           ffn1_w_up, ffn1_w_down, ffn2_w_up, ffn2_w_down, w_in, sinks,
           w_branch_ret, w_branch_swa, w_out):
    batch, seq, d = x.shape
    n = batch * seq
    assert d == D_MODEL and seq % max(TOKEN_TILE, MERGE_TILE, RET_TILE, SWA_TILE) == 0
    xf = x.reshape(n, d)

    pad_rows = -batch % 8
    c_pad = jnp.pad(c, ((0, pad_rows), (0, 0)))
    mod, rope = _prep(c_pad, w_ada, b_ada, positions)
    mod = mod[:, :batch].reshape(DEPTH * batch, N_MOD, D_MODEL)

    for l in range(DEPTH):
        xf = _ffn(xf, norm_ffn1[l], mod, l, 0, ffn1_w_up, ffn1_w_down, batch)
        rq, rk, rv, rg, sq, skv, gr, gs = _proj(xf, norm_mix[l], mod, l, rope, w_in, batch)
        ret = _retention(rq, rk, rv, rg, batch)
        swa = _swa(sinks[l], sq, skv, batch)
        xf = _merge(xf, mod, l, ret, swa, gr, gs, w_branch_ret, w_branch_swa, w_out, batch)
        final_gain = final_norm if l == DEPTH - 1 else None
        xf = _ffn(xf, norm_ffn2[l], mod, l, 6, ffn2_w_up, ffn2_w_down, batch, final_gain)
    return xf.reshape(batch, seq, d)
```

```python
import functools

import numpy as np
import jax
import jax.numpy as jnp
from jax import lax
from jax.experimental import pallas as pl
from jax.experimental.pallas import tpu as pltpu

F32 = jnp.float32
BF16 = jnp.bfloat16

D_MODEL = 1024
DEPTH = 2
RET_HEADS = 4
RET_DK = 128
RET_DV = 256
CHUNK = 128
SWA_Q_HEADS = 16
SWA_KV_HEADS = 2
SWA_HEAD_DIM = 64
D_FF = 2816
ROPE_THETA = 10000.0
NORM_EPS = 1e-6
N_MOD = 9
LOG2_E = 1.4426950408889634

RET_FREQS = RET_DK // 2
SWA_FREQS = SWA_HEAD_DIM // 2
RET_QK_WIDTH = RET_HEADS * RET_DK
RET_V_WIDTH = RET_HEADS * RET_DV
SWA_Q_WIDTH = SWA_Q_HEADS * SWA_HEAD_DIM
SWA_KV_WIDTH = SWA_KV_HEADS * SWA_HEAD_DIM
OFF_RQ = 0
OFF_RK = OFF_RQ + RET_QK_WIDTH
OFF_RV = OFF_RK + RET_QK_WIDTH
OFF_RG = OFF_RV + RET_V_WIDTH
OFF_SQ = OFF_RG + RET_V_WIDTH
OFF_SK = OFF_SQ + SWA_Q_WIDTH
OFF_SV = OFF_SK + SWA_KV_WIDTH
OFF_GR = OFF_SV + SWA_KV_WIDTH
OFF_GS = OFF_GR + D_MODEL
IN_COLS = OFF_GS + D_MODEL

LANES = 128
ROPE_COLS = 5 * LANES

TOKEN_TILE = 512
MERGE_TILE = 1024
RET_TILE = 2048
SWA_TILE = 2048
FF_CHUNK = 256
FFN_DOWN_PARTS = 4
PROJ_SEGMENTS = 8
NORM_ROWS = 32
RET_BLOCK = 256
ADA_COLS = 2304
VMEM_LIMIT = 56 * 1024 * 1024


def _sigmoid(v):
    return 1.0 / (1.0 + jnp.exp(-v))


def _dot(a, b):
    return jnp.dot(a, b, preferred_element_type=F32)


def _dot_nt(a, b):
    return lax.dot_general(a, b, (((1,), (1,)), ((), ())), preferred_element_type=F32)


def _resident(shape):
    zeros = (0,) * len(shape)
    return pl.BlockSpec(shape, lambda *_: zeros, pipeline_mode=pl.Buffered(1))


def _layer_resident(layer, shape):
    zeros = (0,) * len(shape)
    return pl.BlockSpec((None,) + tuple(shape), lambda *_: (layer,) + zeros,
                        pipeline_mode=pl.Buffered(1))


def _wdot(a, w):
    return jnp.dot(a, w.astype(BF16), preferred_element_type=F32)


def _params():
    return pltpu.CompilerParams(dimension_semantics=("arbitrary",), vmem_limit_bytes=VMEM_LIMIT)


def _params2():
    return pltpu.CompilerParams(dimension_semantics=("arbitrary", "arbitrary"),
                                vmem_limit_bytes=VMEM_LIMIT)


def _norm_mod(x, gain, shift, scale):
    y = x * lax.rsqrt(jnp.mean(x * x, axis=-1, keepdims=True) + NORM_EPS)
    return (y * gain) * (1.0 + scale) + shift


def _ada_kernel(c_ref, w_ref, b_ref, o_ref):
    c = c_ref[...]
    act = (c * _sigmoid(c)).astype(BF16)
    o_ref[...] = _dot(act, w_ref[...].astype(BF16)) + b_ref[...]


def _rope_kernel(pos_ref, invf_ref, o_ref):
    assert 2 * RET_FREQS == LANES and 2 * SWA_FREQS == SWA_HEAD_DIM
    half = pos_ref.shape[0] // 2
    lane = lax.broadcasted_iota(jnp.int32, (half, LANES), 1)
    low = lane < RET_FREQS
    pos = jnp.where(low, pos_ref[0:half, :], pos_ref[half:, :])
    ang = pos * invf_ref[...]
    c = jnp.cos(ang)
    s = jnp.sin(ang)
    c_other = pltpu.roll(c, RET_FREQS, 1)
    s_other = pltpu.roll(s, RET_FREQS, 1)
    first_half = (lane & (SWA_HEAD_DIM - 1)) < SWA_FREQS
    even = 2 * (lane & (SWA_FREQS - 1))
    for t in range(2):
        rows = slice(t * half, (t + 1) * half)
        mine = low if t == 0 else jnp.logical_not(low)
        o_ref[rows, 0:LANES] = jnp.where(mine, c, c_other)
        o_ref[rows, LANES:2 * LANES] = jnp.where(low, -1.0, 1.0) * jnp.where(mine, s, s_other)
        cs = jnp.take_along_axis(c, even + RET_FREQS * t, axis=1)
        ss = jnp.take_along_axis(s, even + RET_FREQS * t, axis=1)
        o_ref[rows, 2 * LANES:3 * LANES] = cs
        o_ref[rows, 3 * LANES:4 * LANES] = jnp.where(first_half, -ss, 0.0)
        o_ref[rows, 4 * LANES:5 * LANES] = jnp.where(first_half, 0.0, ss)


def _prep_kernel(c_ref, w_ref, b_ref, pos_ref, invf_ref, mod_ref, rope_ref):
    _ada_kernel(c_ref, w_ref, b_ref, mod_ref)
    _rope_kernel(pos_ref, invf_ref, rope_ref)


def _prep(c_pad, w_ada, b_ada, positions):
    rows = c_pad.shape[0]
    cols = N_MOD * D_MODEL
    blocks = cols // ADA_COLS
    steps = DEPTH * blocks
    n = positions.size
    assert n % steps == 0
    tm = n // steps
    inv_r = 1.0 / (ROPE_THETA ** (jnp.arange(0, RET_DK, 2, dtype=F32) / RET_DK))
    invf = jnp.concatenate([inv_r, inv_r]).reshape(1, LANES)
    pos = positions.astype(F32).reshape(n, 1)
    return pl.pallas_call(
        _prep_kernel,
        grid=(steps,),
        in_specs=[
            pl.BlockSpec((rows, D_MODEL), lambda i: (0, 0)),
            pl.BlockSpec((None, D_MODEL, ADA_COLS), lambda i: (i // blocks, 0, i % blocks)),
            pl.BlockSpec((None, 1, ADA_COLS), lambda i: (i // blocks, 0, i % blocks)),
            pl.BlockSpec((tm, 1), lambda i: (i, 0)),
            pl.BlockSpec((1, LANES), lambda i: (0, 0)),
        ],
        out_specs=[pl.BlockSpec((None, rows, ADA_COLS), lambda i: (i // blocks, 0, i % blocks)),
                   pl.BlockSpec((tm, ROPE_COLS), lambda i: (i, 0))],
        out_shape=[jax.ShapeDtypeStruct((DEPTH, rows, cols), F32),
                   jax.ShapeDtypeStruct((n, ROPE_COLS), F32)],
        compiler_params=_params(),
        name="prep",
    )(c_pad, w_ada, b_ada.reshape(DEPTH, 1, cols), pos, invf)


def _ffn_kernel(layer, mod_row, final, x_ref, xn_ref, gain_ref, mod_ref, modn_ref, wup_hbm, wdn_hbm, *rest):
    if final:
        fgain_ref, o_ref, act_ref, h_ref, wup_ref, wdn_ref, sem = rest
    else:
        o_ref, act_ref, h_ref, wup_ref, wdn_ref, sem = rest
    n_up = D_FF // FF_CHUNK
    dn_rows = D_FF // FFN_DOWN_PARTS

    def up_copy(j, half):
        cols = pl.ds(half * D_FF + j * FF_CHUNK, FF_CHUNK)
        return pltpu.make_async_copy(wup_hbm.at[layer, :, cols], wup_ref.at[:, cols],
                                     sem.at[2 * j + half])

    def down_copy(p):
        rows = pl.ds(p * dn_rows, dn_rows)
        return pltpu.make_async_copy(wdn_hbm.at[layer, rows, :], wdn_ref.at[rows, :],
                                     sem.at[2 * n_up + p])

    def normed(xr, mr):
        return _norm_mod(xr[...], gain_ref[...], mr[mod_row:mod_row + 1, :],
                         mr[mod_row + 1:mod_row + 2, :]).astype(BF16)

    def up_chunk(j):
        lo = j * FF_CHUNK
        g = _wdot(h_ref[...], wup_ref[:, lo:lo + FF_CHUNK])
        u = _wdot(h_ref[...], wup_ref[:, D_FF + lo:D_FF + lo + FF_CHUNK])
        act_ref[:, lo:lo + FF_CHUNK] = (g * _sigmoid(g) * u).astype(BF16)

    def finish_tile():
        y = _wdot(act_ref[...], wdn_ref[...])
        h_ref[...] = normed(xn_ref, modn_ref)
        out = x_ref[...] + (0.5 * mod_ref[mod_row + 2:mod_row + 3, :]) * y
        if final:
            out = out * lax.rsqrt(jnp.mean(out * out, axis=-1, keepdims=True) + NORM_EPS)
            out = out * fgain_ref[...]
        o_ref[...] = out
        up_chunk(0)

    @pl.when(pl.program_id(0) == 0)
    def _():
        for j in range(n_up):
            up_copy(j, 0).start()
            up_copy(j, 1).start()
        for p in range(FFN_DOWN_PARTS):
            down_copy(p).start()
        h_ref[...] = normed(x_ref, mod_ref)
        for j in range(n_up):
            up_copy(j, 0).wait()
            up_copy(j, 1).wait()
            up_chunk(j)
        for p in range(FFN_DOWN_PARTS):
            down_copy(p).wait()
        finish_tile()

    @pl.when(pl.program_id(0) > 0)
    def _():
        for j in range(1, n_up):
            up_chunk(j)
        finish_tile()


def _tile_specs(n, tm, batch, layer):
    tiles = n // tm
    tiles_per_batch = tiles // batch
    nxt = lambda i: jnp.minimum(i + 1, tiles - 1)
    return [
        pl.BlockSpec((tm, D_MODEL), lambda i: (i, 0)),
        pl.BlockSpec((tm, D_MODEL), lambda i: (nxt(i), 0)),
        _resident((1, D_MODEL)),
        pl.BlockSpec((None, N_MOD, D_MODEL), lambda i: (layer * batch + i // tiles_per_batch, 0, 0)),
        pl.BlockSpec((None, N_MOD, D_MODEL), lambda i: (layer * batch + nxt(i) // tiles_per_batch, 0, 0)),
    ]


def _ffn(x, gain, mod, layer, mod_row, w_up, w_down, batch, final_gain=None):
    n = x.shape[0]
    tm = TOKEN_TILE
    final = final_gain is not None
    in_specs = _tile_specs(n, tm, batch, layer) + [
        pl.BlockSpec(memory_space=pl.ANY),
        pl.BlockSpec(memory_space=pl.ANY),
    ]
    args = [x, x, gain.reshape(1, D_MODEL), mod, mod, w_up, w_down]
    if final:
        in_specs.append(_resident((1, D_MODEL)))
        args.append(final_gain.reshape(1, D_MODEL))
    return pl.pallas_call(
        functools.partial(_ffn_kernel, layer, mod_row, final),
        grid=(n // tm,),
        in_specs=in_specs,
        out_specs=pl.BlockSpec((tm, D_MODEL), lambda i: (i, 0)),
        out_shape=jax.ShapeDtypeStruct((n, D_MODEL), F32),
        scratch_shapes=[pltpu.VMEM((tm, D_FF), BF16), pltpu.VMEM((tm, D_MODEL), BF16),
                        pltpu.VMEM((D_MODEL, 2 * D_FF), F32), pltpu.VMEM((D_FF, D_MODEL), F32),
                        pltpu.SemaphoreType.DMA((2 * (D_FF // FF_CHUNK) + FFN_DOWN_PARTS,))],
        compiler_params=_params(),
        name="ffn",
    )(*args)


def _ret_tables():
    h = np.arange(RET_HEADS, dtype=np.float64)
    log_gamma = np.log1p(-(2.0 ** (-5.0 - h)))
    idx = np.arange(RET_BLOCK, dtype=np.float64)
    rel = idx[:, None] - idx[None, :]
    decay = np.where(rel[None] >= 0, np.exp(np.maximum(rel, 0.0)[None] * log_gamma[:, None, None]), 0.0)
    decay = decay * (RET_DK ** -0.5)
    zeta = np.exp((RET_BLOCK - 1 - idx)[None, :] * log_gamma[:, None])
    xi = np.exp((idx + 1)[None, :] * log_gamma[:, None]) * (RET_DK ** -0.5)
    zeta = np.broadcast_to(zeta[:, :, None], (RET_HEADS, RET_BLOCK, RET_DK))
    xi = np.broadcast_to(xi[:, :, None], (RET_HEADS, RET_BLOCK, RET_DK))
    gamma_block = tuple(float(v) for v in np.exp(RET_BLOCK * log_gamma))
    return (jnp.asarray(decay, F32), jnp.asarray(zeta, F32), jnp.asarray(xi, F32), gamma_block)


def _retention_tile(first, gamma_block, q_ref, k_ref, v_ref, g_ref, dec_ref, zeta_ref, xi_ref,
                    o_ref, state_ref, raw_ref):
    rows_total = q_ref.shape[0]
    for c in range(rows_total // RET_BLOCK):
        rows = slice(c * RET_BLOCK, (c + 1) * RET_BLOCK)
        for h in range(RET_HEADS):
            kcols = slice(h * RET_DK, (h + 1) * RET_DK)
            vcols = slice(h * RET_DV, (h + 1) * RET_DV)
            q = q_ref[rows, kcols]
            k = k_ref[rows, kcols]
            v = v_ref[rows, vcols]
            state = state_ref[h]
            if c == 0:
                state = jnp.where(first, 0.0, state)
            scores = _dot_nt(q, k) * dec_ref[h]
            q_in = q.astype(F32) * xi_ref[h]
            kz = (k.astype(F32) * zeta_ref[h]).astype(BF16)
            lhs = jnp.concatenate(
                [jnp.concatenate([scores.astype(BF16), q_in.astype(BF16)], axis=1),
                 jnp.concatenate([kz.T, jnp.zeros((RET_DK, RET_DK), BF16)], axis=1)], axis=0)
            rhs = jnp.concatenate([v, state.astype(BF16)], axis=0)
            res = _dot(lhs, rhs)
            raw_ref[rows, vcols] = res[:RET_BLOCK]
            state_ref[h] = state * gamma_block[h] + res[RET_BLOCK:]

    for r in range(rows_total // NORM_ROWS):
        rows = slice(r * NORM_ROWS, (r + 1) * NORM_ROWS)
        for h in range(RET_HEADS):
            vcols = slice(h * RET_DV, (h + 1) * RET_DV)
            out = raw_ref[rows, vcols]
            mu = jnp.mean(out, axis=-1, keepdims=True)
            dev = out - mu
            var = jnp.mean(dev * dev, axis=-1, keepdims=True)
            gate = g_ref[rows, vcols].astype(F32)
            o_ref[rows, vcols] = (gate * _sigmoid(gate) * (dev * lax.rsqrt(var + NORM_EPS))).astype(BF16)


def _swa_tile(first, sink_ref, q_ref, kv_ref, kvp_ref, o_ref):
    low = lax.broadcasted_iota(jnp.int32, (CHUNK, LANES), 1) < SWA_HEAD_DIM
    qi = lax.broadcasted_iota(jnp.int32, (CHUNK, CHUNK), 0)
    kj = lax.broadcasted_iota(jnp.int32, (CHUNK, CHUNK), 1)
    tri = kj <= qi
    zero = jnp.zeros((CHUNK, LANES), F32)
    ones_lo = jnp.where(low, 1.0, 0.0)
    ones_hi = jnp.where(low, 0.0, 1.0)

    def pieces(kvb):
        kf = kvb[:, :LANES].astype(F32)
        vf = kvb[:, LANES:].astype(F32)
        ks = pltpu.roll(kf, SWA_HEAD_DIM, 1)
        vs = pltpu.roll(vf, SWA_HEAD_DIM, 1)
        out = []
        for lo_src_k, hi_src_k, lo_src_v, hi_src_v in ((kf, ks, vf, vs), (ks, kf, vs, vf)):
            k_lo = jnp.where(low, lo_src_k, zero).astype(BF16)
            k_hi = jnp.where(low, zero, hi_src_k).astype(BF16)
            vo_lo = jnp.concatenate([jnp.where(low, lo_src_v, zero), ones_lo], axis=1).astype(BF16)
            vo_hi = jnp.concatenate([jnp.where(low, zero, hi_src_v), ones_hi], axis=1).astype(BF16)
            out.append((k_lo, k_hi, vo_lo, vo_hi))
        return out

    prev = pieces(kvp_ref[...])
    for j in range(q_ref.shape[0] // CHUNK):
        rows = slice(j * CHUNK, (j + 1) * CHUNK)
        cur = pieces(kv_ref[rows, :])
        kb = [jnp.concatenate([prev[g][0], cur[g][0], prev[g][1], cur[g][1]], axis=0)
              for g in range(SWA_KV_HEADS)]
        vo = [jnp.concatenate([prev[g][2], cur[g][2], prev[g][3], cur[g][3]], axis=0)
              for g in range(SWA_KV_HEADS)]
        pairs_per_group = SWA_Q_HEADS // 2 // SWA_KV_HEADS
        for g in range(SWA_KV_HEADS):
            pairs = range(g * pairs_per_group, (g + 1) * pairs_per_group)
            q_all = jnp.concatenate([q_ref[rows, p * LANES:(p + 1) * LANES] for p in pairs], axis=0)
            s_all = _dot_nt(q_all, kb[g])
            probs_all = []
            sink_all = []
            for i, p in enumerate(pairs):
                s = s_all[i * CHUNK:(i + 1) * CHUNK]
                probs = []
                sink_terms = []
                for hh in range(2):
                    sink = sink_ref[2 * p + hh] * LOG2_E
                    s_prev = s[:, (2 * hh) * CHUNK:(2 * hh + 1) * CHUNK]
                    s_cur = s[:, (2 * hh + 1) * CHUNK:(2 * hh + 2) * CHUNK]
                    if j == 0:
                        s_prev = jnp.where(first, -jnp.inf, s_prev)
                    comb = jnp.where(tri, s_cur, s_prev)
                    m = jnp.maximum(jnp.max(comb, axis=-1, keepdims=True), sink)
                    pe = jnp.exp2(comb - m)
                    probs += [jnp.where(tri, 0.0, pe), jnp.where(tri, pe, 0.0)]
                    sink_terms.append(jnp.exp2(sink - m))
                probs_all.append(jnp.concatenate(probs, axis=1).astype(BF16))
                sink_all.append(jnp.where(low, sink_terms[0], sink_terms[1]))
            res_all = _dot(jnp.concatenate(probs_all, axis=0), vo[g])
            for i, p in enumerate(pairs):
                res = res_all[i * CHUNK:(i + 1) * CHUNK]
                den = res[:, LANES:] + sink_all[i]
                o_ref[rows, p * LANES:(p + 1) * LANES] = (res[:, :LANES] / den).astype(BF16)
        prev = cur


def _proj_kernel(layer, x_ref, xn_ref, gain_ref, mod_ref, modn_ref, rope_ref, w_hbm,
                 rq_ref, rk_ref, rv_ref, rg_ref, sq_ref, skv_ref, gr_ref, gs_ref,
                 h_ref, carry_ref, w_ref, sem):
    def normed(xr, mr):
        return _norm_mod(xr[...], gain_ref[...], mr[3:4, :], mr[4:5, :]).astype(BF16)

    cos_r = rope_ref[:, 0:LANES]
    sin_r = rope_ref[:, LANES:2 * LANES]
    cos_s = rope_ref[:, 2 * LANES:3 * LANES]
    sin_a = rope_ref[:, 3 * LANES:4 * LANES]
    sin_b = rope_ref[:, 4 * LANES:5 * LANES]

    def rope_ret(v):
        return v * cos_r + pltpu.roll(v, RET_FREQS, 1) * sin_r

    def rope_swa(v):
        return (v * cos_s + pltpu.roll(v, LANES - SWA_FREQS, 1) * sin_a
                + pltpu.roll(v, SWA_FREQS, 1) * sin_b)

    def rope_swa_q(v):
        return rope_swa(v) * (SWA_HEAD_DIM ** -0.5 * LOG2_E)

    def project(off, width, out_ref, out_off=0, rope=None):
        y = _wdot(h_ref[...], w_ref[:, off:off + width])
        if rope is None:
            out_ref[:, out_off:out_off + width] = y.astype(out_ref.dtype)
        else:
            for s in range(width // LANES):
                slab = y[:, s * LANES:(s + 1) * LANES]
                out_ref[:, out_off + s * LANES:out_off + (s + 1) * LANES] = rope(slab).astype(out_ref.dtype)

    def seg_skv():
        project(OFF_SK, SWA_KV_WIDTH, skv_ref, rope=rope_swa)
        project(OFF_SV, SWA_KV_WIDTH, skv_ref, out_off=SWA_KV_WIDTH)

    segments = (
        (OFF_RV, RET_V_WIDTH, None),
        (OFF_RQ, RET_QK_WIDTH, lambda: project(OFF_RQ, RET_QK_WIDTH, rq_ref, rope=rope_ret)),
        (OFF_RK, RET_QK_WIDTH, lambda: project(OFF_RK, RET_QK_WIDTH, rk_ref, rope=rope_ret)),
        (OFF_SQ, SWA_Q_WIDTH, lambda: project(OFF_SQ, SWA_Q_WIDTH, sq_ref, rope=rope_swa_q)),
        (OFF_SK, 2 * SWA_KV_WIDTH, seg_skv),
        (OFF_RG, RET_V_WIDTH, lambda: project(OFF_RG, RET_V_WIDTH, rg_ref)),
        (OFF_GR, D_MODEL, lambda: project(OFF_GR, D_MODEL, gr_ref)),
        (OFF_GS, D_MODEL, lambda: project(OFF_GS, D_MODEL, gs_ref)),
    )
    assert len(segments) == PROJ_SEGMENTS

    def w_copy(s):
        cols = pl.ds(segments[s][0], segments[s][1])
        return pltpu.make_async_copy(w_hbm.at[layer, :, cols], w_ref.at[:, cols], sem.at[s])

    def tile_body(wait):
        rv_ref[...] = carry_ref[...]
        for s in range(1, len(segments)):
            if wait:
                w_copy(s).wait()
            segments[s][2]()
        h_ref[...] = normed(xn_ref, modn_ref)
        project(OFF_RV, RET_V_WIDTH, carry_ref)

    @pl.when(pl.program_id(0) == 0)
    def _():
        for s in range(len(segments)):
            w_copy(s).start()
        h_ref[...] = normed(x_ref, mod_ref)
        w_copy(0).wait()
        project(OFF_RV, RET_V_WIDTH, carry_ref)
        tile_body(wait=True)

    @pl.when(pl.program_id(0) > 0)
    def _():
        tile_body(wait=False)


def _proj(x, gain, mod, layer, rope, w_in, batch):
    n = x.shape[0]
    tm = TOKEN_TILE
    widths = (RET_QK_WIDTH, RET_QK_WIDTH, RET_V_WIDTH, RET_V_WIDTH, SWA_Q_WIDTH,
              2 * SWA_KV_WIDTH, D_MODEL, D_MODEL)
    return pl.pallas_call(
        functools.partial(_proj_kernel, layer),
        grid=(n // tm,),
        in_specs=_tile_specs(n, tm, batch, layer) + [
            pl.BlockSpec((tm, ROPE_COLS), lambda i: (i, 0)),
            pl.BlockSpec(memory_space=pl.ANY),
        ],
        out_specs=[pl.BlockSpec((tm, w), lambda i: (i, 0)) for w in widths],
        out_shape=[jax.ShapeDtypeStruct((n, w), BF16) for w in widths],
        scratch_shapes=[pltpu.VMEM((tm, D_MODEL), BF16), pltpu.VMEM((tm, RET_V_WIDTH), BF16),
                        pltpu.VMEM((D_MODEL, IN_COLS), F32), pltpu.SemaphoreType.DMA((PROJ_SEGMENTS,))],
        compiler_params=_params(),
        name="proj",
    )(x, x, gain.reshape(1, D_MODEL), mod, mod, rope, w_in)


def _ret_kernel(gamma_block, q_ref, k_ref, v_ref, g_ref, dec_ref, zeta_ref, xi_ref, o_ref, state_ref,
                raw_ref):
    _retention_tile(pl.program_id(1) == 0, gamma_block, q_ref, k_ref, v_ref, g_ref, dec_ref, zeta_ref,
                    xi_ref, o_ref, state_ref, raw_ref)


def _retention(rq, rk, rv, rg, batch):
    n = rq.shape[0]
    steps = n // batch // RET_TILE
    decay, zeta, xi, gamma_block = _ret_tables()
    row_block = lambda w: pl.BlockSpec((RET_TILE, w), lambda b, t: (b * steps + t, 0))
    return pl.pallas_call(
        functools.partial(_ret_kernel, gamma_block),
        grid=(batch, steps),
        in_specs=[row_block(RET_QK_WIDTH), row_block(RET_QK_WIDTH), row_block(RET_V_WIDTH),
                  row_block(RET_V_WIDTH),
                  _resident((RET_HEADS, RET_BLOCK, RET_BLOCK)), _resident((RET_HEADS, RET_BLOCK, RET_DK)),
                  _resident((RET_HEADS, RET_BLOCK, RET_DK))],
        out_specs=row_block(RET_V_WIDTH),
        out_shape=jax.ShapeDtypeStruct((n, RET_V_WIDTH), BF16),
        scratch_shapes=[pltpu.VMEM((RET_HEADS, RET_DK, RET_DV), F32),
                        pltpu.VMEM((RET_TILE, RET_V_WIDTH), F32)],
        compiler_params=_params2(),
        name="retention",
    )(rq, rk, rv, rg, decay, zeta, xi)


def _swa_kernel(sink_ref, q_ref, kv_ref, kvp_ref, o_ref):
    _swa_tile(pl.program_id(1) == 0, sink_ref, q_ref, kv_ref, kvp_ref, o_ref)


def _swa(sinks, sq, skv, batch):
    n = sq.shape[0]
    steps = n // batch // SWA_TILE
    per = SWA_TILE // CHUNK
    return pl.pallas_call(
        _swa_kernel,
        grid=(batch, steps),
        in_specs=[
            pl.BlockSpec(memory_space=pltpu.SMEM),
            pl.BlockSpec((SWA_TILE, SWA_Q_WIDTH), lambda b, t: (b * steps + t, 0)),
            pl.BlockSpec((SWA_TILE, 2 * SWA_KV_WIDTH), lambda b, t: (b * steps + t, 0)),
            pl.BlockSpec((CHUNK, 2 * SWA_KV_WIDTH),
                         lambda b, t: ((b * steps + t) * per - jnp.minimum(t, 1), 0)),
        ],
        out_specs=pl.BlockSpec((SWA_TILE, SWA_Q_WIDTH), lambda b, t: (b * steps + t, 0)),
        out_shape=jax.ShapeDtypeStruct((n, SWA_Q_WIDTH), BF16),
        compiler_params=_params2(),
        name="swa",
    )(sinks, sq, skv, skv)


def _merge_kernel(x_ref, mod_ref, ret_ref, swa_ref, gr_ref, gs_ref, wr_ref, ws_ref, wo_ref, o_ref):
    br = _wdot(ret_ref[...], wr_ref[...])
    bs = _wdot(swa_ref[...], ws_ref[...])
    merged = _sigmoid(gr_ref[...].astype(F32)) * br + _sigmoid(gs_ref[...].astype(F32)) * bs
    y = _wdot(merged.astype(BF16), wo_ref[...])
    o_ref[...] = x_ref[...] + mod_ref[5:6, :] * y


def _merge(x, mod, layer, ret, swa, gr, gs, w_ret, w_swa, w_out, batch):
    n = x.shape[0]
    tm = MERGE_TILE
    tiles_per_batch = n // batch // tm
    rows = lambda w: pl.BlockSpec((tm, w), lambda i: (i, 0))
    return pl.pallas_call(
        _merge_kernel,
        grid=(n // tm,),
        in_specs=[
            rows(D_MODEL),
            pl.BlockSpec((None, N_MOD, D_MODEL), lambda i: (layer * batch + i // tiles_per_batch, 0, 0)),
            rows(RET_V_WIDTH), rows(SWA_Q_WIDTH), rows(D_MODEL), rows(D_MODEL),
            _layer_resident(layer, (RET_V_WIDTH, D_MODEL)), _layer_resident(layer, (SWA_Q_WIDTH, D_MODEL)),
            _layer_resident(layer, (D_MODEL, D_MODEL)),
        ],
        out_specs=rows(D_MODEL),
        out_shape=jax.ShapeDtypeStruct((n, D_MODEL), F32),
        compiler_params=_params(),
        name="merge",
    )(x, mod, ret, swa, gr, gs, w_ret, w_swa, w_out)


def kernel(x, c, positions, norm_ffn1, norm_mix, norm_ffn2, final_norm, w_ada, b_ada,
           ffn1_w_up, ffn1_w_down, ffn2_w_up, ffn2_w_down, w_in, sinks,
           w_branch_ret, w_branch_swa, w_out):
    batch, seq, d = x.shape
    n = batch * seq
    assert d == D_MODEL and seq % max(TOKEN_TILE, MERGE_TILE, RET_TILE, SWA_TILE) == 0
    xf = x.reshape(n, d)

    pad_rows = -batch % 8
    c_pad = jnp.pad(c, ((0, pad_rows), (0, 0)))
    mod, rope = _prep(c_pad, w_ada, b_ada, positions)
    mod = mod[:, :batch].reshape(DEPTH * batch, N_MOD, D_MODEL)

    for l in range(DEPTH):
        xf = _ffn(xf, norm_ffn1[l], mod, l, 0, ffn1_w_up, ffn1_w_down, batch)
        rq, rk, rv, rg, sq, skv, gr, gs = _proj(xf, norm_mix[l], mod, l, rope, w_in, batch)
        ret = _retention(rq, rk, rv, rg, batch)
        swa = _swa(sinks[l], sq, skv, batch)
        xf = _merge(xf, mod, l, ret, swa, gr, gs, w_branch_ret, w_branch_swa, w_out, batch)
        final_gain = final_norm if l == DEPTH - 1 else None
        xf = _ffn(xf, norm_ffn2[l], mod, l, 6, ffn2_w_up, ffn2_w_down, batch, final_gain)
    return xf.reshape(batch, seq, d)
```

```python
import functools

import numpy as np
import jax
import jax.numpy as jnp
from jax import lax
from jax.experimental import pallas as pl
from jax.experimental.pallas import tpu as pltpu

F32 = jnp.float32
BF16 = jnp.bfloat16

D_MODEL = 1024
DEPTH = 2
RET_HEADS = 4
RET_DK = 128
RET_DV = 256
CHUNK = 128
SWA_Q_HEADS = 16
SWA_KV_HEADS = 2
SWA_HEAD_DIM = 64
D_FF = 2816
ROPE_THETA = 10000.0
NORM_EPS = 1e-6
N_MOD = 9
LOG2_E = 1.4426950408889634

RET_FREQS = RET_DK // 2
SWA_FREQS = SWA_HEAD_DIM // 2
RET_QK_WIDTH = RET_HEADS * RET_DK
RET_V_WIDTH = RET_HEADS * RET_DV
SWA_Q_WIDTH = SWA_Q_HEADS * SWA_HEAD_DIM
SWA_KV_WIDTH = SWA_KV_HEADS * SWA_HEAD_DIM
OFF_RQ = 0
OFF_RK = OFF_RQ + RET_QK_WIDTH
OFF_RV = OFF_RK + RET_QK_WIDTH
OFF_RG = OFF_RV + RET_V_WIDTH
OFF_SQ = OFF_RG + RET_V_WIDTH
OFF_SK = OFF_SQ + SWA_Q_WIDTH
OFF_SV = OFF_SK + SWA_KV_WIDTH
OFF_GR = OFF_SV + SWA_KV_WIDTH
OFF_GS = OFF_GR + D_MODEL
IN_COLS = OFF_GS + D_MODEL

LANES = 128
ROPE_COLS = 5 * LANES

TOKEN_TILE = 512
MERGE_TILE = 1024
RET_TILE = 1024
SWA_TILE = 2048
FF_CHUNK = 256
FFN_DOWN_PARTS = 4
PROJ_SEGMENTS = 8
NORM_ROWS = 32
RET_BLOCK = 256
ADA_COLS = 1152
VMEM_LIMIT = 56 * 1024 * 1024
MERGE_VMEM_LIMIT = 59 * 1024 * 1024


def _sigmoid(v):
    return 1.0 / (1.0 + jnp.exp(-v))


def _dot(a, b):
    return jnp.dot(a, b, preferred_element_type=F32)


def _dot_nt(a, b):
    return lax.dot_general(a, b, (((1,), (1,)), ((), ())), preferred_element_type=F32)


def _resident(shape):
    zeros = (0,) * len(shape)
    return pl.BlockSpec(shape, lambda *_: zeros, pipeline_mode=pl.Buffered(1))


def _layer_resident(layer, shape):
    zeros = (0,) * len(shape)
    return pl.BlockSpec((None,) + tuple(shape), lambda *_: (layer,) + zeros,
                        pipeline_mode=pl.Buffered(1))


def _wdot(a, w):
    return jnp.dot(a, w.astype(BF16), preferred_element_type=F32)


def _params(vmem_limit=VMEM_LIMIT):
    return pltpu.CompilerParams(dimension_semantics=("arbitrary",), vmem_limit_bytes=vmem_limit)


def _params2():
    return pltpu.CompilerParams(dimension_semantics=("arbitrary", "arbitrary"),
                                vmem_limit_bytes=VMEM_LIMIT)


def _norm_mod(x, gain, shift, scale):
    y = x * lax.rsqrt(jnp.mean(x * x, axis=-1, keepdims=True) + NORM_EPS)
    return (y * gain) * (1.0 + scale) + shift


def _ada_kernel(c_ref, w_ref, b_ref, o_ref):
    c = c_ref[...]
    act = (c * _sigmoid(c)).astype(BF16)
    o_ref[...] = _dot(act, w_ref[...].astype(BF16)) + b_ref[...]


def _rope_kernel(pos_ref, invf_ref, o_ref):
    assert 2 * RET_FREQS == LANES and 2 * SWA_FREQS == SWA_HEAD_DIM
    half = pos_ref.shape[0] // 2
    lane = lax.broadcasted_iota(jnp.int32, (half, LANES), 1)
    low = lane < RET_FREQS
    pos = jnp.where(low, pos_ref[0:half, :], pos_ref[half:, :])
    ang = pos * invf_ref[...]
    c = jnp.cos(ang)
    s = jnp.sin(ang)
    c_other = pltpu.roll(c, RET_FREQS, 1)
    s_other = pltpu.roll(s, RET_FREQS, 1)
    first_half = (lane & (SWA_HEAD_DIM - 1)) < SWA_FREQS
    even = 2 * (lane & (SWA_FREQS - 1))
    for t in range(2):
        rows = slice(t * half, (t + 1) * half)
        mine = low if t == 0 else jnp.logical_not(low)
        o_ref[rows, 0:LANES] = jnp.where(mine, c, c_other)
        o_ref[rows, LANES:2 * LANES] = jnp.where(low, -1.0, 1.0) * jnp.where(mine, s, s_other)
        cs = jnp.take_along_axis(c, even + RET_FREQS * t, axis=1)
        ss = jnp.take_along_axis(s, even + RET_FREQS * t, axis=1)
        o_ref[rows, 2 * LANES:3 * LANES] = cs
        o_ref[rows, 3 * LANES:4 * LANES] = jnp.where(first_half, -ss, 0.0)
        o_ref[rows, 4 * LANES:5 * LANES] = jnp.where(first_half, 0.0, ss)


def _prep_kernel(c_ref, w_ref, b_ref, pos_ref, invf_ref, mod_ref, rope_ref):
    _ada_kernel(c_ref, w_ref, b_ref, mod_ref)
    _rope_kernel(pos_ref, invf_ref, rope_ref)


def _prep(c_pad, w_ada, b_ada, positions):
    rows = c_pad.shape[0]
    cols = N_MOD * D_MODEL
    blocks = cols // ADA_COLS
    steps = DEPTH * blocks
    n = positions.size
    assert n % steps == 0
    tm = n // steps
    inv_r = 1.0 / (ROPE_THETA ** (jnp.arange(0, RET_DK, 2, dtype=F32) / RET_DK))
    invf = jnp.concatenate([inv_r, inv_r]).reshape(1, LANES)
    pos = positions.astype(F32).reshape(n, 1)
    return pl.pallas_call(
        _prep_kernel,
        grid=(steps,),
        in_specs=[
            pl.BlockSpec((rows, D_MODEL), lambda i: (0, 0)),
            pl.BlockSpec((None, D_MODEL, ADA_COLS), lambda i: (i // blocks, 0, i % blocks)),
            pl.BlockSpec((None, 1, ADA_COLS), lambda i: (i // blocks, 0, i % blocks)),
            pl.BlockSpec((tm, 1), lambda i: (i, 0)),
            pl.BlockSpec((1, LANES), lambda i: (0, 0)),
        ],
        out_specs=[pl.BlockSpec((None, rows, ADA_COLS), lambda i: (i // blocks, 0, i % blocks)),
                   pl.BlockSpec((tm, ROPE_COLS), lambda i: (i, 0))],
        out_shape=[jax.ShapeDtypeStruct((DEPTH, rows, cols), F32),
                   jax.ShapeDtypeStruct((n, ROPE_COLS), F32)],
        compiler_params=_params(),
        name="prep",
    )(c_pad, w_ada, b_ada.reshape(DEPTH, 1, cols), pos, invf)


def _ffn_kernel(layer, mod_row, final, x_ref, xn_ref, gain_ref, mod_ref, modn_ref, wup_hbm, wdn_hbm, *rest):
    if final:
        fgain_ref, o_ref, act_ref, h_ref, wup_ref, wdn_ref, sem = rest
    else:
        o_ref, act_ref, h_ref, wup_ref, wdn_ref, sem = rest
    n_up = D_FF // FF_CHUNK
    dn_rows = D_FF // FFN_DOWN_PARTS

    def up_copy(j, half):
        cols = pl.ds(half * D_FF + j * FF_CHUNK, FF_CHUNK)
        return pltpu.make_async_copy(wup_hbm.at[layer, :, cols], wup_ref.at[:, cols],
                                     sem.at[2 * j + half])

    def down_copy(p):
        rows = pl.ds(p * dn_rows, dn_rows)
        return pltpu.make_async_copy(wdn_hbm.at[layer, rows, :], wdn_ref.at[rows, :],
                                     sem.at[2 * n_up + p])

    def normed(xr, mr):
        return _norm_mod(xr[...], gain_ref[...], mr[mod_row:mod_row + 1, :],
                         mr[mod_row + 1:mod_row + 2, :]).astype(BF16)

    def up_chunk(j):
        lo = j * FF_CHUNK
        g = _wdot(h_ref[...], wup_ref[:, lo:lo + FF_CHUNK])
        u = _wdot(h_ref[...], wup_ref[:, D_FF + lo:D_FF + lo + FF_CHUNK])
        act_ref[:, lo:lo + FF_CHUNK] = (g * _sigmoid(g) * u).astype(BF16)

    def finish_tile():
        y = _wdot(act_ref[...], wdn_ref[...])
        h_ref[...] = normed(xn_ref, modn_ref)
        out = x_ref[...] + (0.5 * mod_ref[mod_row + 2:mod_row + 3, :]) * y
        if final:
            out = out * lax.rsqrt(jnp.mean(out * out, axis=-1, keepdims=True) + NORM_EPS)
            out = out * fgain_ref[...]
        o_ref[...] = out
        up_chunk(0)

    @pl.when(pl.program_id(0) == 0)
    def _():
        for j in range(n_up):
            up_copy(j, 0).start()
            up_copy(j, 1).start()
        for p in range(FFN_DOWN_PARTS):
            down_copy(p).start()
        h_ref[...] = normed(x_ref, mod_ref)
        for j in range(n_up):
            up_copy(j, 0).wait()
            up_copy(j, 1).wait()
            up_chunk(j)
        for p in range(FFN_DOWN_PARTS):
            down_copy(p).wait()
        finish_tile()

    @pl.when(pl.program_id(0) > 0)
    def _():
        for j in range(1, n_up):
            up_chunk(j)
        finish_tile()


def _tile_specs(n, tm, batch, layer):
    tiles = n // tm
    tiles_per_batch = tiles // batch
    nxt = lambda i: jnp.minimum(i + 1, tiles - 1)
    return [
        pl.BlockSpec((tm, D_MODEL), lambda i: (i, 0)),
        pl.BlockSpec((tm, D_MODEL), lambda i: (nxt(i), 0)),
        _resident((1, D_MODEL)),
        pl.BlockSpec((None, N_MOD, D_MODEL), lambda i: (layer * batch + i // tiles_per_batch, 0, 0)),
        pl.BlockSpec((None, N_MOD, D_MODEL), lambda i: (layer * batch + nxt(i) // tiles_per_batch, 0, 0)),
    ]


def _ffn(x, gain, mod, layer, mod_row, w_up, w_down, batch, final_gain=None):
    n = x.shape[0]
    tm = TOKEN_TILE
    final = final_gain is not None
    in_specs = _tile_specs(n, tm, batch, layer) + [
        pl.BlockSpec(memory_space=pl.ANY),
        pl.BlockSpec(memory_space=pl.ANY),
    ]
    args = [x, x, gain.reshape(1, D_MODEL), mod, mod, w_up, w_down]
    if final:
        in_specs.append(_resident((1, D_MODEL)))
        args.append(final_gain.reshape(1, D_MODEL))
    return pl.pallas_call(
        functools.partial(_ffn_kernel, layer, mod_row, final),
        grid=(n // tm,),
        in_specs=in_specs,
        out_specs=pl.BlockSpec((tm, D_MODEL), lambda i: (i, 0)),
        out_shape=jax.ShapeDtypeStruct((n, D_MODEL), F32),
        scratch_shapes=[pltpu.VMEM((tm, D_FF), BF16), pltpu.VMEM((tm, D_MODEL), BF16),
                        pltpu.VMEM((D_MODEL, 2 * D_FF), F32), pltpu.VMEM((D_FF, D_MODEL), F32),
                        pltpu.SemaphoreType.DMA((2 * (D_FF // FF_CHUNK) + FFN_DOWN_PARTS,))],
        compiler_params=_params(),
        name="ffn",
    )(*args)


def _ret_tables():
    h = np.arange(RET_HEADS, dtype=np.float64)
    log_gamma = np.log1p(-(2.0 ** (-5.0 - h)))
    idx = np.arange(RET_BLOCK, dtype=np.float64)
    rel = idx[:, None] - idx[None, :]
    decay = np.where(rel[None] >= 0, np.exp(np.maximum(rel, 0.0)[None] * log_gamma[:, None, None]), 0.0)
    decay = decay * (RET_DK ** -0.5)
    zeta = np.exp((RET_BLOCK - 1 - idx)[None, :] * log_gamma[:, None])
    xi = np.exp((idx + 1)[None, :] * log_gamma[:, None]) * (RET_DK ** -0.5)
    zeta = np.broadcast_to(zeta[:, :, None], (RET_HEADS, RET_BLOCK, RET_DK))
    xi = np.broadcast_to(xi[:, :, None], (RET_HEADS, RET_BLOCK, RET_DK))
    gamma_block = tuple(float(v) for v in np.exp(RET_BLOCK * log_gamma))
    return (jnp.asarray(decay, F32), jnp.asarray(zeta, F32), jnp.asarray(xi, F32), gamma_block)


def _retention_tile(first, gamma_block, q_ref, k_ref, v_ref, g_ref, dec_ref, zeta_ref, xi_ref,
                    o_ref, state_ref, raw_ref):
    rows_total = q_ref.shape[0]
    for c in range(rows_total // RET_BLOCK):
        rows = slice(c * RET_BLOCK, (c + 1) * RET_BLOCK)
        for h in range(RET_HEADS):
            kcols = slice(h * RET_DK, (h + 1) * RET_DK)
            vcols = slice(h * RET_DV, (h + 1) * RET_DV)
            q = q_ref[rows, kcols]
            k = k_ref[rows, kcols]
            v = v_ref[rows, vcols]
            state = state_ref[h]
            if c == 0:
                state = jnp.where(first, 0.0, state)
            scores = _dot_nt(q, k) * dec_ref[h]
            q_in = q.astype(F32) * xi_ref[h]
            kz = (k.astype(F32) * zeta_ref[h]).astype(BF16)
            lhs = jnp.concatenate(
                [jnp.concatenate([scores.astype(BF16), q_in.astype(BF16)], axis=1),
                 jnp.concatenate([kz.T, jnp.zeros((RET_DK, RET_DK), BF16)], axis=1)], axis=0)
            rhs = jnp.concatenate([v, state.astype(BF16)], axis=0)
            res = _dot(lhs, rhs)
            raw_ref[rows, vcols] = res[:RET_BLOCK]
            state_ref[h] = state * gamma_block[h] + res[RET_BLOCK:]

    for r in range(rows_total // NORM_ROWS):
        rows = slice(r * NORM_ROWS, (r + 1) * NORM_ROWS)
        for h in range(RET_HEADS):
            vcols = slice(h * RET_DV, (h + 1) * RET_DV)
            out = raw_ref[rows, vcols]
            mu = jnp.mean(out, axis=-1, keepdims=True)
            dev = out - mu
            var = jnp.mean(dev * dev, axis=-1, keepdims=True)
            gate = g_ref[rows, vcols].astype(F32)
            o_ref[rows, vcols] = (gate * _sigmoid(gate) * (dev * lax.rsqrt(var + NORM_EPS))).astype(BF16)


def _swa_tile(first, sink_ref, q_ref, kv_ref, kvp_ref, o_ref):
    low = lax.broadcasted_iota(jnp.int32, (CHUNK, LANES), 1) < SWA_HEAD_DIM
    qi = lax.broadcasted_iota(jnp.int32, (CHUNK, CHUNK), 0)
    kj = lax.broadcasted_iota(jnp.int32, (CHUNK, CHUNK), 1)
    tri = kj <= qi
    zero = jnp.zeros((CHUNK, LANES), F32)
    ones_lo = jnp.where(low, 1.0, 0.0)
    ones_hi = jnp.where(low, 0.0, 1.0)

    def pieces(kvb):
        kf = kvb[:, :LANES].astype(F32)
        vf = kvb[:, LANES:].astype(F32)
        ks = pltpu.roll(kf, SWA_HEAD_DIM, 1)
        vs = pltpu.roll(vf, SWA_HEAD_DIM, 1)
        out = []
        for lo_src_k, hi_src_k, lo_src_v, hi_src_v in ((kf, ks, vf, vs), (ks, kf, vs, vf)):
            k_lo = jnp.where(low, lo_src_k, zero).astype(BF16)
            k_hi = jnp.where(low, zero, hi_src_k).astype(BF16)
            vo_lo = jnp.concatenate([jnp.where(low, lo_src_v, zero), ones_lo], axis=1).astype(BF16)
            vo_hi = jnp.concatenate([jnp.where(low, zero, hi_src_v), ones_hi], axis=1).astype(BF16)
            out.append((k_lo, k_hi, vo_lo, vo_hi))
        return out

    prev = pieces(kvp_ref[...])
    for j in range(q_ref.shape[0] // CHUNK):
        rows = slice(j * CHUNK, (j + 1) * CHUNK)
        cur = pieces(kv_ref[rows, :])
        kb = [jnp.concatenate([prev[g][0], cur[g][0], prev[g][1], cur[g][1]], axis=0)
              for g in range(SWA_KV_HEADS)]
        vo = [jnp.concatenate([prev[g][2], cur[g][2], prev[g][3], cur[g][3]], axis=0)
              for g in range(SWA_KV_HEADS)]
        pairs_per_group = SWA_Q_HEADS // 2 // SWA_KV_HEADS
        for g in range(SWA_KV_HEADS):
            pairs = range(g * pairs_per_group, (g + 1) * pairs_per_group)
            q_all = jnp.concatenate([q_ref[rows, p * LANES:(p + 1) * LANES] for p in pairs], axis=0)
            s_all = _dot_nt(q_all, kb[g])
            probs_all = []
            sink_all = []
            for i, p in enumerate(pairs):
                s = s_all[i * CHUNK:(i + 1) * CHUNK]
                probs = []
                sink_terms = []
                for hh in range(2):
                    sink = sink_ref[2 * p + hh] * LOG2_E
                    s_prev = s[:, (2 * hh) * CHUNK:(2 * hh + 1) * CHUNK]
                    s_cur = s[:, (2 * hh + 1) * CHUNK:(2 * hh + 2) * CHUNK]
                    if j == 0:
                        s_prev = jnp.where(first, -jnp.inf, s_prev)
                    comb = jnp.where(tri, s_cur, s_prev)
                    m = jnp.maximum(jnp.max(comb, axis=-1, keepdims=True), sink)
                    pe = jnp.exp2(comb - m)
                    probs += [jnp.where(tri, 0.0, pe), jnp.where(tri, pe, 0.0)]
                    sink_terms.append(jnp.exp2(sink - m))
                probs_all.append(jnp.concatenate(probs, axis=1).astype(BF16))
                sink_all.append(jnp.where(low, sink_terms[0], sink_terms[1]))
            res_all = _dot(jnp.concatenate(probs_all, axis=0), vo[g])
            for i, p in enumerate(pairs):
                res = res_all[i * CHUNK:(i + 1) * CHUNK]
                den = res[:, LANES:] + sink_all[i]
                o_ref[rows, p * LANES:(p + 1) * LANES] = (res[:, :LANES] / den).astype(BF16)
        prev = cur


def _proj_kernel(layer, x_ref, xn_ref, gain_ref, mod_ref, modn_ref, rope_ref, w_hbm,
                 rq_ref, rk_ref, rv_ref, rg_ref, sq_ref, skv_ref, gr_ref, gs_ref,
                 h_ref, carry_ref, w_ref, sem):
    def normed(xr, mr):
        return _norm_mod(xr[...], gain_ref[...], mr[3:4, :], mr[4:5, :]).astype(BF16)

    cos_r = rope_ref[:, 0:LANES]
    sin_r = rope_ref[:, LANES:2 * LANES]
    cos_s = rope_ref[:, 2 * LANES:3 * LANES]
    sin_a = rope_ref[:, 3 * LANES:4 * LANES]
    sin_b = rope_ref[:, 4 * LANES:5 * LANES]

    def rope_ret(v):
        return v * cos_r + pltpu.roll(v, RET_FREQS, 1) * sin_r

    def rope_swa(v):
        return (v * cos_s + pltpu.roll(v, LANES - SWA_FREQS, 1) * sin_a
                + pltpu.roll(v, SWA_FREQS, 1) * sin_b)

    def rope_swa_q(v):
        return rope_swa(v) * (SWA_HEAD_DIM ** -0.5 * LOG2_E)

    def project(off, width, out_ref, out_off=0, rope=None):
        y = _wdot(h_ref[...], w_ref[:, off:off + width])
        if rope is None:
            out_ref[:, out_off:out_off + width] = y.astype(out_ref.dtype)
        else:
            for s in range(width // LANES):
                slab = y[:, s * LANES:(s + 1) * LANES]
                out_ref[:, out_off + s * LANES:out_off + (s + 1) * LANES] = rope(slab).astype(out_ref.dtype)

    def seg_skv():
        project(OFF_SK, SWA_KV_WIDTH, skv_ref, rope=rope_swa)
        project(OFF_SV, SWA_KV_WIDTH, skv_ref, out_off=SWA_KV_WIDTH)

    segments = (
        (OFF_RV, RET_V_WIDTH, None),
        (OFF_RQ, RET_QK_WIDTH, lambda: project(OFF_RQ, RET_QK_WIDTH, rq_ref, rope=rope_ret)),
        (OFF_RK, RET_QK_WIDTH, lambda: project(OFF_RK, RET_QK_WIDTH, rk_ref, rope=rope_ret)),
        (OFF_SQ, SWA_Q_WIDTH, lambda: project(OFF_SQ, SWA_Q_WIDTH, sq_ref, rope=rope_swa_q)),
        (OFF_SK, 2 * SWA_KV_WIDTH, seg_skv),
        (OFF_RG, RET_V_WIDTH, lambda: project(OFF_RG, RET_V_WIDTH, rg_ref)),
        (OFF_GR, D_MODEL, lambda: project(OFF_GR, D_MODEL, gr_ref)),
        (OFF_GS, D_MODEL, lambda: project(OFF_GS, D_MODEL, gs_ref)),
    )
    assert len(segments) == PROJ_SEGMENTS

    def w_copy(s):
        cols = pl.ds(segments[s][0], segments[s][1])
        return pltpu.make_async_copy(w_hbm.at[layer, :, cols], w_ref.at[:, cols], sem.at[s])

    def tile_body(wait):
        rv_ref[...] = carry_ref[...]
        for s in range(1, len(segments)):
            if wait:
                w_copy(s).wait()
            segments[s][2]()
        h_ref[...] = normed(xn_ref, modn_ref)
        project(OFF_RV, RET_V_WIDTH, carry_ref)

    @pl.when(pl.program_id(0) == 0)
    def _():
        for s in range(len(segments)):
            w_copy(s).start()
        h_ref[...] = normed(x_ref, mod_ref)
        w_copy(0).wait()
        project(OFF_RV, RET_V_WIDTH, carry_ref)
        tile_body(wait=True)

    @pl.when(pl.program_id(0) > 0)
    def _():
        tile_body(wait=False)


def _proj(x, gain, mod, layer, rope, w_in, batch):
    n = x.shape[0]
    tm = TOKEN_TILE
    widths = (RET_QK_WIDTH, RET_QK_WIDTH, RET_V_WIDTH, RET_V_WIDTH, SWA_Q_WIDTH,
              2 * SWA_KV_WIDTH, D_MODEL, D_MODEL)
    return pl.pallas_call(
        functools.partial(_proj_kernel, layer),
        grid=(n // tm,),
        in_specs=_tile_specs(n, tm, batch, layer) + [
            pl.BlockSpec((tm, ROPE_COLS), lambda i: (i, 0)),
            pl.BlockSpec(memory_space=pl.ANY),
        ],
        out_specs=[pl.BlockSpec((tm, w), lambda i: (i, 0)) for w in widths],
        out_shape=[jax.ShapeDtypeStruct((n, w), BF16) for w in widths],
        scratch_shapes=[pltpu.VMEM((tm, D_MODEL), BF16), pltpu.VMEM((tm, RET_V_WIDTH), BF16),
                        pltpu.VMEM((D_MODEL, IN_COLS), F32), pltpu.SemaphoreType.DMA((PROJ_SEGMENTS,))],
        compiler_params=_params(),
        name="proj",
    )(x, x, gain.reshape(1, D_MODEL), mod, mod, rope, w_in)


def _ret_kernel(gamma_block, q_ref, k_ref, v_ref, g_ref, dec_ref, zeta_ref, xi_ref, o_ref, state_ref,
                raw_ref):
    _retention_tile(pl.program_id(1) == 0, gamma_block, q_ref, k_ref, v_ref, g_ref, dec_ref, zeta_ref,
                    xi_ref, o_ref, state_ref, raw_ref)


def _retention(rq, rk, rv, rg, batch):
    n = rq.shape[0]
    steps = n // batch // RET_TILE
    decay, zeta, xi, gamma_block = _ret_tables()
    row_block = lambda w: pl.BlockSpec((RET_TILE, w), lambda b, t: (b * steps + t, 0))
    return pl.pallas_call(
        functools.partial(_ret_kernel, gamma_block),
        grid=(batch, steps),
        in_specs=[row_block(RET_QK_WIDTH), row_block(RET_QK_WIDTH), row_block(RET_V_WIDTH),
                  row_block(RET_V_WIDTH),
                  _resident((RET_HEADS, RET_BLOCK, RET_BLOCK)), _resident((RET_HEADS, RET_BLOCK, RET_DK)),
                  _resident((RET_HEADS, RET_BLOCK, RET_DK))],
        out_specs=row_block(RET_V_WIDTH),
        out_shape=jax.ShapeDtypeStruct((n, RET_V_WIDTH), BF16),
        scratch_shapes=[pltpu.VMEM((RET_HEADS, RET_DK, RET_DV), F32),
                        pltpu.VMEM((RET_TILE, RET_V_WIDTH), F32)],
        compiler_params=_params2(),
        name="retention",
    )(rq, rk, rv, rg, decay, zeta, xi)


def _swa_kernel(sink_ref, q_ref, kv_ref, kvp_ref, o_ref):
    _swa_tile(pl.program_id(1) == 0, sink_ref, q_ref, kv_ref, kvp_ref, o_ref)


def _swa(sinks, sq, skv, batch):
    n = sq.shape[0]
    steps = n // batch // SWA_TILE
    per = SWA_TILE // CHUNK
    return pl.pallas_call(
        _swa_kernel,
        grid=(batch, steps),
        in_specs=[
            pl.BlockSpec(memory_space=pltpu.SMEM),
            pl.BlockSpec((SWA_TILE, SWA_Q_WIDTH), lambda b, t: (b * steps + t, 0)),
            pl.BlockSpec((SWA_TILE, 2 * SWA_KV_WIDTH), lambda b, t: (b * steps + t, 0)),
            pl.BlockSpec((CHUNK, 2 * SWA_KV_WIDTH),
                         lambda b, t: ((b * steps + t) * per - jnp.minimum(t, 1), 0)),
        ],
        out_specs=pl.BlockSpec((SWA_TILE, SWA_Q_WIDTH), lambda b, t: (b * steps + t, 0)),
        out_shape=jax.ShapeDtypeStruct((n, SWA_Q_WIDTH), BF16),
        compiler_params=_params2(),
        name="swa",
    )(sinks, sq, skv, skv)


def _merge_kernel(x_ref, mod_ref, ret_ref, retn_ref, swa_ref, gr_ref, gs_ref, wr_ref, ws_ref, wo_ref,
                  o_ref, br_ref):
    @pl.when(pl.program_id(0) == 0)
    def _():
        br_ref[...] = _wdot(ret_ref[...], wr_ref[...])

    bs = _wdot(swa_ref[...], ws_ref[...])
    merged = _sigmoid(gr_ref[...].astype(F32)) * br_ref[...] + _sigmoid(gs_ref[...].astype(F32)) * bs
    y = _wdot(merged.astype(BF16), wo_ref[...])
    o_ref[...] = x_ref[...] + mod_ref[5:6, :] * y
    br_ref[...] = _wdot(retn_ref[...], wr_ref[...])


def _merge(x, mod, layer, ret, swa, gr, gs, w_ret, w_swa, w_out, batch):
    n = x.shape[0]
    tm = MERGE_TILE
    tiles = n // tm
    tiles_per_batch = tiles // batch
    rows = lambda w: pl.BlockSpec((tm, w), lambda i: (i, 0))
    return pl.pallas_call(
        _merge_kernel,
        grid=(tiles,),
        in_specs=[
            rows(D_MODEL),
            pl.BlockSpec((None, N_MOD, D_MODEL), lambda i: (layer * batch + i // tiles_per_batch, 0, 0)),
            rows(RET_V_WIDTH),
            pl.BlockSpec((tm, RET_V_WIDTH), lambda i: (jnp.minimum(i + 1, tiles - 1), 0)),
            rows(SWA_Q_WIDTH), rows(D_MODEL), rows(D_MODEL),
            _layer_resident(layer, (RET_V_WIDTH, D_MODEL)), _layer_resident(layer, (SWA_Q_WIDTH, D_MODEL)),
            _layer_resident(layer, (D_MODEL, D_MODEL)),
        ],
        out_specs=rows(D_MODEL),
        out_shape=jax.ShapeDtypeStruct((n, D_MODEL), F32),
        scratch_shapes=[pltpu.VMEM((tm, D_MODEL), F32)],
        compiler_params=_params(MERGE_VMEM_LIMIT),
        name="merge",
    )(x, mod, ret, ret, swa, gr, gs, w_ret, w_swa, w_out)


def kernel(x, c, positions, norm_ffn1, norm_mix, norm_ffn2, final_norm, w_ada, b_ada,
           ffn1_w_up, ffn1_w_down, ffn2_w_up, ffn2_w_down, w_in, sinks,
           w_branch_ret, w_branch_swa, w_out):
    batch, seq, d = x.shape
    n = batch * seq
    assert d == D_MODEL and seq % max(TOKEN_TILE, MERGE_TILE, RET_TILE, SWA_TILE) == 0
    xf = x.reshape(n, d)

    pad_rows = -batch % 8
    c_pad = jnp.pad(c, ((0, pad_rows), (0, 0)))
    mod, rope = _prep(c_pad, w_ada, b_ada, positions)
    mod = mod[:, :batch].reshape(DEPTH * batch, N_MOD, D_MODEL)

    for l in range(DEPTH):
        xf = _ffn(xf, norm_ffn1[l], mod, l, 0, ffn1_w_up, ffn1_w_down, batch)
        rq, rk, rv, rg, sq, skv, gr, gs = _proj(xf, norm_mix[l], mod, l, rope, w_in, batch)
        ret = _retention(rq, rk, rv, rg, batch)
        swa = _swa(sinks[l], sq, skv, batch)
        xf = _merge(xf, mod, l, ret, swa, gr, gs, w_branch_ret, w_branch_swa, w_out, batch)
        final_gain = final_norm if l == DEPTH - 1 else None
        xf = _ffn(xf, norm_ffn2[l], mod, l, 6, ffn2_w_up, ffn2_w_down, batch, final_gain)
    return xf.reshape(batch, seq, d)
```
